```python
import math
import jax, jax.numpy as jnp
from jax import lax
import numpy as np

D_MODEL = 1024
BATCH = 4
SEQ = 4096
DEPTH = 2

CHUNK = 64
Q_BLOCK = 128
HEAD_DIM = 64
N_HEADS_DIFF = 4
DIFF_V_DIM = 2 * HEAD_DIM
N_HEADS_SB = 8
N_HEADS_FOX = 8
W_DIFF = N_HEADS_DIFF * DIFF_V_DIM
W_SB = N_HEADS_SB * HEAD_DIM
W_FOX = N_HEADS_FOX * HEAD_DIM
D_MIX = W_DIFF + W_SB + W_FOX
ROT_DIM = HEAD_DIM // 4
ROPE_THETA = 500000.0
NORM_EPS = 1e-6
IN_SPLITS = [W_DIFF, W_DIFF, W_DIFF, W_DIFF,
             W_SB, W_SB, W_SB, W_SB,
             W_FOX, W_FOX, W_FOX, W_FOX,
             N_HEADS_FOX]
D_IN = sum(IN_SPLITS)

kernel_name = "hybrid_diff_stickbreak_fox_block"


def rms_norm(x, gain):
    xf = x.astype(jnp.float32)
    y = xf * lax.rsqrt(jnp.mean(xf * xf, axis=-1, keepdims=True) + NORM_EPS)
    return (y * gain.astype(jnp.float32)).astype(x.dtype)


def rope_tables(seq):
    pos = jnp.arange(seq, dtype=jnp.float32)
    inv_freq = ROPE_THETA ** (-jnp.arange(0, ROT_DIM, 2, dtype=jnp.float32) / ROT_DIM)
    ang = pos[:, None] * inv_freq[None, :]
    return jnp.cos(ang), jnp.sin(ang)


def apply_partial_rope(t, cos, sin):
    rot, rest = t[..., :ROT_DIM], t[..., ROT_DIM:]
    r1, r2 = rot[..., :ROT_DIM // 2], rot[..., ROT_DIM // 2:]
    rot = jnp.concatenate([r1 * cos - r2 * sin, r2 * cos + r1 * sin], axis=-1).astype(t.dtype)
    return jnp.concatenate([rot, rest], axis=-1)


def diff_attention(q, k, v, lam, lambda_init, sub_gain):
    seq = q.shape[3]
    scale = q.shape[-1] ** -0.5
    chunk_id = jnp.arange(seq) // CHUNK
    outs = []
    for i in range(seq // Q_BLOCK):
        q0, q1 = i * Q_BLOCK, (i + 1) * Q_BLOCK
        s = jnp.einsum('bhcqd,bhckd->bhcqk', q[:, :, :, q0:q1], k[:, :, :, :q1]).astype(jnp.float32) * scale
        allowed = chunk_id[:q1][None, :] <= chunk_id[q0:q1][:, None]
        p = jax.nn.softmax(jnp.where(allowed, s, -jnp.inf), axis=-1)
        w = p[:, :, 0] - lam * p[:, :, 1]
        outs.append(jnp.einsum('bhqk,bhkd->bhqd', w, v[:, :, :q1].astype(jnp.float32)))
    o = jnp.concatenate(outs, axis=2)
    return rms_norm(o, sub_gain) * (1.0 - lambda_init)


def stick_breaking_attention(q, k, v):
    seq = q.shape[2]
    scale = q.shape[-1] ** -0.5
    pos = jnp.arange(seq)
    outs = []
    for i in range(seq // Q_BLOCK):
        q0, q1 = i * Q_BLOCK, (i + 1) * Q_BLOCK
        z = jnp.einsum('bhqd,bhkd->bhqk', q[:, :, q0:q1], k[:, :, :q1]).astype(jnp.float32) * scale
        strict = pos[:q1][None, :] < pos[q0:q1][:, None]
        log_keep = jnp.where(strict, jax.nn.log_sigmoid(-z), 0.0)
        log_rest = lax.cumsum(log_keep, axis=3, reverse=True) - log_keep
        a = jnp.where(strict, jnp.exp(jax.nn.log_sigmoid(z) + log_rest), 0.0)
        outs.append(jnp.einsum('bhqk,bhkd->bhqd', a, v[:, :, :q1].astype(jnp.float32)))
    return jnp.concatenate(outs, axis=2)


def forgetting_attention(q, k, v, cum_log_f):
    seq = q.shape[2]
    scale = q.shape[-1] ** -0.5
    pos = jnp.arange(seq)
    outs = []
    for i in range(seq // Q_BLOCK):
        q0, q1 = i * Q_BLOCK, (i + 1) * Q_BLOCK
        logits = jnp.einsum('bhqd,bhkd->bhqk', q[:, :, q0:q1], k[:, :, :q1]).astype(jnp.float32) * scale
        logits = logits + cum_log_f[:, :, q0:q1, None] - cum_log_f[:, :, None, :q1]
        causal = pos[:q1][None, :] <= pos[q0:q1][:, None]
        p = jax.nn.softmax(jnp.where(causal, logits, -jnp.inf), axis=-1)
        outs.append(jnp.einsum('bhqk,bhkd->bhqd', p, v[:, :, :q1].astype(jnp.float32)))
    return jnp.concatenate(outs, axis=2)


def hybrid_layer(x, w_in, f_bias, lam_vec, subln, w_out, g_pre, g_post, layer_idx, cos, sin):
    b, s, _ = x.shape
    dt = x.dtype
    h = rms_norm(x, g_pre)
    p = h @ w_in
    split_at = np.cumsum(IN_SPLITS)[:-1].tolist()
    qa, ka, va, ga, qs, ks, vs, gs, qf, kf, vf, gf, ff = jnp.split(p, split_at, axis=-1)

    def heads(t, n, d):
        return t.reshape(b, s, n, d).transpose(0, 2, 1, 3)

    def merge(t):
        return t.transpose(0, 2, 1, 3).reshape(b, s, -1).astype(dt)

    qa = apply_partial_rope(qa.reshape(b, s, N_HEADS_DIFF, 2, HEAD_DIM).transpose(0, 2, 3, 1, 4), cos, sin)
    ka = apply_partial_rope(ka.reshape(b, s, N_HEADS_DIFF, 2, HEAD_DIM).transpose(0, 2, 3, 1, 4), cos, sin)
    lambda_init = 0.8 - 0.6 * math.exp(-0.3 * layer_idx)
    lv = lam_vec.astype(jnp.float32)
    lam = jnp.exp(jnp.sum(lv[0] * lv[1])) - jnp.exp(jnp.sum(lv[2] * lv[3])) + lambda_init
    ya = merge(diff_attention(qa, ka, heads(va, N_HEADS_DIFF, DIFF_V_DIM), lam, lambda_init, subln))
    ya = ya * jax.nn.silu(ga)

    yb = merge(stick_breaking_attention(heads(qs, N_HEADS_SB, HEAD_DIM), heads(ks, N_HEADS_SB, HEAD_DIM),
                                        heads(vs, N_HEADS_SB, HEAD_DIM)))
    yb = yb * jax.nn.silu(gs)

    log_f = jax.nn.log_sigmoid((ff + f_bias).astype(jnp.float32)).transpose(0, 2, 1)
    cum_log_f = jnp.cumsum(log_f, axis=-1)
    yc = merge(forgetting_attention(heads(qf, N_HEADS_FOX, HEAD_DIM), heads(kf, N_HEADS_FOX, HEAD_DIM),
                                    heads(vf, N_HEADS_FOX, HEAD_DIM), cum_log_f))
    yc = yc * jax.nn.silu(gf)

    y = jnp.concatenate([ya, yb, yc], axis=-1) @ w_out
    return x + rms_norm(y, g_post)


def setup_inputs(seed: int = 0) -> dict:
    key = jax.random.key(seed)
    ks = jax.random.split(key, 8)
    x = jax.random.normal(ks[0], (BATCH, SEQ, D_MODEL), jnp.float32)
    w_in = jax.random.normal(ks[1], (DEPTH, D_MODEL, D_IN), jnp.float32) * D_MODEL ** -0.5
    forget_bias = jax.random.uniform(ks[2], (DEPTH, N_HEADS_FOX), jnp.float32, minval=1.0, maxval=4.0)
    diff_lambda = 0.1 * jax.random.normal(ks[3], (DEPTH, 4, HEAD_DIM), jnp.float32)
    diff_subln = 1.0 + 0.02 * jax.random.normal(ks[4], (DEPTH, DIFF_V_DIM), jnp.float32)
    w_out = jax.random.normal(ks[5], (DEPTH, D_MIX, D_MODEL), jnp.float32) * D_MIX ** -0.5
    pre_norm = 1.0 + 0.02 * jax.random.normal(ks[6], (DEPTH, D_MODEL), jnp.float32)
    post_norm = 1.0 + 0.02 * jax.random.normal(ks[7], (DEPTH, D_MODEL), jnp.float32)
    return {"x": x, "w_in": w_in, "forget_bias": forget_bias, "diff_lambda": diff_lambda,
            "diff_subln": diff_subln, "w_out": w_out, "pre_norm": pre_norm, "post_norm": post_norm}


def reference(x, w_in, forget_bias, diff_lambda, diff_subln, w_out, pre_norm, post_norm):
    cos, sin = rope_tables(x.shape[1])
    for l in range(DEPTH):
        x = hybrid_layer(x, w_in[l], forget_bias[l], diff_lambda[l], diff_subln[l], w_out[l],
                         pre_norm[l], post_norm[l], l, cos, sin)
    return x
```

```python
import functools
import math

import jax
import jax.numpy as jnp
from jax import lax
from jax.experimental import pallas as pl
from jax.experimental.pallas import tpu as pltpu

F32 = jnp.float32
BF16 = jnp.bfloat16

D_MODEL = 1024
HEAD_DIM = 64
CHUNK = 64
ROT_DIM = HEAD_DIM // 4
ROPE_THETA = 500000.0
NORM_EPS = 1e-6
W_GROUP = 512
N_HEADS_FOX = 8
D_MAIN = 12 * W_GROUP
D_MIX = 3 * W_GROUP

LANES = 128
GROUP_BLOCKS = W_GROUP // LANES
VMEM_LIMIT = 56 * 1024 * 1024

TM_IN = 1024
TN_IN = 1024
TM_OUT = 512
TQ = 256
TK = 256
CS_BLOCK = 256
NEG_BIG = -1e30


def _dot(a, b):
    return jnp.dot(a, b, preferred_element_type=F32)


def _dot_nt(a, b):
    return lax.dot_general(a, b, (((1,), (1,)), ((), ())), preferred_element_type=F32)


def _softplus(z):
    return jnp.maximum(z, 0.0) + jnp.log(1.0 + jnp.exp(-jnp.abs(z)))


def _silu(g):
    return g / (1.0 + jnp.exp(-g))


def _in_proj_kernel(x_ref, g_ref, w_ref, wf_ref, rc_ref, rs1_ref, rs2_ref, p_ref, ff_ref, h_ref):
    j = pl.program_id(1)

    @pl.when(j == 0)
    def _():
        x = x_ref[...]
        ms = jnp.mean(x * x, axis=-1, keepdims=True)
        h = (x * lax.rsqrt(ms + NORM_EPS) * g_ref[...]).astype(BF16)
        h_ref[...] = h
        ff_ref[...] = _dot(h, wf_ref[...])

    acc = _dot(h_ref[...], w_ref[...])

    @pl.when(j == 0)
    def _():
        rc, rs1, rs2 = rc_ref[...], rs1_ref[...], rs2_ref[...]
        for c in range(TN_IN // LANES):
            a = acc[:, c * LANES:(c + 1) * LANES]
            r = a * rc + pltpu.roll(a, ROT_DIM // 2, 1) * rs1 + pltpu.roll(a, LANES - ROT_DIM // 2, 1) * rs2
            p_ref[:, c * LANES:(c + 1) * LANES] = r.astype(BF16)

    @pl.when(j != 0)
    def _():
        p_ref[...] = acc.astype(BF16)


def _in_proj(x2, g_pre, w_main, w_f, rc, rs1, rs2, seq):
    m = x2.shape[0]
    n_pos_tiles = seq // TM_IN
    return pl.pallas_call(
        _in_proj_kernel,
        grid=(m // TM_IN, D_MAIN // TN_IN),
        in_specs=[
            pl.BlockSpec((TM_IN, D_MODEL), lambda i, j: (i, 0)),
            pl.BlockSpec((1, D_MODEL), lambda i, j: (0, 0)),
            pl.BlockSpec((D_MODEL, TN_IN), lambda i, j: (0, j)),
            pl.BlockSpec((D_MODEL, LANES), lambda i, j: (0, 0)),
            pl.BlockSpec((TM_IN, LANES), lambda i, j: (i % n_pos_tiles, 0)),
            pl.BlockSpec((TM_IN, LANES), lambda i, j: (i % n_pos_tiles, 0)),
            pl.BlockSpec((TM_IN, LANES), lambda i, j: (i % n_pos_tiles, 0)),
        ],
        out_specs=[
            pl.BlockSpec((TM_IN, TN_IN), lambda i, j: (i, j)),
            pl.BlockSpec((TM_IN, LANES), lambda i, j: (i, 0)),
        ],
        out_shape=[
            jax.ShapeDtypeStruct((m, D_MAIN), BF16),
            jax.ShapeDtypeStruct((m, LANES), F32),
        ],
        scratch_shapes=[pltpu.VMEM((TM_IN, D_MODEL), BF16)],
        compiler_params=pltpu.CompilerParams(
            dimension_semantics=("arbitrary", "arbitrary"), vmem_limit_bytes=VMEM_LIMIT),
        name="in_proj",
    )(x2, g_pre, w_main, w_f, rc, rs1, rs2)


def _cumsum_kernel(ff_ref, b_ref, o_ref):
    seq = ff_ref.shape[1]
    row = lax.broadcasted_iota(jnp.int32, (CS_BLOCK, CS_BLOCK), 0)
    col = lax.broadcasted_iota(jnp.int32, (CS_BLOCK, CS_BLOCK), 1)
    tri = (col <= row).astype(BF16)

    def body(c, carry):
        r0 = pl.multiple_of(c * CS_BLOCK, CS_BLOCK)
        z = ff_ref[0, pl.ds(r0, CS_BLOCK), :] + b_ref[...]
        lf = -_softplus(-z)
        hi = lf.astype(BF16)
        r1 = lf - hi.astype(F32)
        mid = r1.astype(BF16)
        lo = (r1 - mid.astype(F32)).astype(BF16)
        cs = _dot(tri, hi) + _dot(tri, mid) + _dot(tri, lo) + carry
        o_ref[0, pl.ds(r0, CS_BLOCK), :] = cs
        return cs[CS_BLOCK - 1:CS_BLOCK, :]

    lax.fori_loop(0, seq // CS_BLOCK, body, jnp.zeros((1, LANES), F32))


def _cumsum(ff3, bias):
    b, seq, _ = ff3.shape
    return pl.pallas_call(
        _cumsum_kernel,
        grid=(b,),
        in_specs=[
            pl.BlockSpec((1, seq, LANES), lambda i: (i, 0, 0)),
            pl.BlockSpec((1, LANES), lambda i: (0, 0)),
        ],
        out_specs=pl.BlockSpec((1, seq, LANES), lambda i: (i, 0, 0)),
        out_shape=jax.ShapeDtypeStruct((b, seq, LANES), F32),
        compiler_params=pltpu.CompilerParams(
            dimension_semantics=("arbitrary",), vmem_limit_bytes=VMEM_LIMIT),
        name="cum_log_forget",
    )(ff3, bias)


def _lane_lo():
    return lax.broadcasted_iota(jnp.int32, (1, LANES), 1) < HEAD_DIM


def _split_heads(q):
    lo = _lane_lo()
    zero = jnp.zeros_like(q)
    return jnp.where(lo, q, zero), jnp.where(lo, zero, q)


def _tile_iotas():
    row = lax.broadcasted_iota(jnp.int32, (TQ, TK), 0)
    col = lax.broadcasted_iota(jnp.int32, (TQ, TK), 1)
    return row, col


def _attn_specs(q_blk, k_blk, v_blk, g_blk, seq):
    return [
        pl.BlockSpec((1, TQ, LANES), lambda b, u, i: (b, i, q_blk + u)),
        pl.BlockSpec((1, seq, LANES), lambda b, u, i: (b, 0, k_blk + u)),
        pl.BlockSpec((1, seq, LANES), lambda b, u, i: (b, 0, v_blk + u)),
        pl.BlockSpec((1, TQ, LANES), lambda b, u, i: (b, i, g_blk + u)),
    ]


_ATTN_PARAMS = pltpu.CompilerParams(
    dimension_semantics=("arbitrary", "arbitrary", "arbitrary"), vmem_limit_bytes=VMEM_LIMIT)


def _diff_kernel(q_ref, k_ref, v_ref, g_ref, lam_ref, sub_ref, o_ref, *, lambda_init):
    i = pl.program_id(2)
    q0, q1 = _split_heads(q_ref[0])

    def scores(kj):
        k0 = pl.multiple_of(kj * TK, TK)
        k = k_ref[0, pl.ds(k0, TK), :]
        v = v_ref[0, pl.ds(k0, TK), :]
        return _dot_nt(q0, k), _dot_nt(q1, k), v

    def update(state, s, v):
        m, l, acc = state
        m_new = jnp.maximum(m, jnp.max(s, axis=-1, keepdims=True))
        alpha = jnp.exp(m - m_new)
        p = jnp.exp(s - m_new)
        l = alpha * l + jnp.sum(p, axis=-1, keepdims=True)
        acc = alpha * acc + _dot(p.astype(BF16), v)
        return m_new, l, acc

    row, col = _tile_iotas()
    allowed = (col // CHUNK) <= (row // CHUNK)
    s0, s1, v = scores(i)
    init = (jnp.full((TQ, 1), NEG_BIG, F32), jnp.zeros((TQ, 1), F32), jnp.zeros((TQ, LANES), F32))
    st0 = update(init, jnp.where(allowed, s0, NEG_BIG), v)
    st1 = update(init, jnp.where(allowed, s1, NEG_BIG), v)

    def body(kj, carry):
        st0, st1 = carry
        s0, s1, v = scores(kj)
        return update(st0, s0, v), update(st1, s1, v)

    st0, st1 = lax.fori_loop(0, i, body, (st0, st1))

    lv = lam_ref[...]
    lam = (jnp.exp(jnp.sum(lv[0:1] * lv[1:2], axis=-1, keepdims=True))
           - jnp.exp(jnp.sum(lv[2:3] * lv[3:4], axis=-1, keepdims=True)) + lambda_init)
    o = st0[2] / st0[1] - lam * (st1[2] / st1[1])
    o = o * lax.rsqrt(jnp.mean(o * o, axis=-1, keepdims=True) + NORM_EPS) * sub_ref[...]
    o = o * (1.0 - lambda_init)
    o_ref[0] = (o * _silu(g_ref[0].astype(F32))).astype(BF16)


def _diff_attention(p3, lam_vec, subln, lambda_init):
    b, seq, _ = p3.shape
    gb = GROUP_BLOCKS
    return pl.pallas_call(
        functools.partial(_diff_kernel, lambda_init=lambda_init),
        grid=(b, gb, seq // TQ),
        in_specs=_attn_specs(0 * gb, 1 * gb, 2 * gb, 3 * gb, seq) + [
            pl.BlockSpec((4, HEAD_DIM), lambda b, u, i: (0, 0)),
            pl.BlockSpec((1, LANES), lambda b, u, i: (0, 0)),
        ],
        out_specs=pl.BlockSpec((1, TQ, LANES), lambda b, u, i: (b, i, u)),
        out_shape=jax.ShapeDtypeStruct((b, seq, W_GROUP), BF16),
        compiler_params=_ATTN_PARAMS,
        name="diff_attention",
    )(p3, p3, p3, p3, lam_vec, subln)


def _sb_kernel(q_ref, k_ref, v_ref, g_ref, o_ref):
    i = pl.program_id(2)
    q0, q1 = _split_heads(q_ref[0])
    lo = _lane_lo()
    row, col = _tile_iotas()
    strict = col < row
    upper = (col < row).astype(BF16)

    def head(qh, k, v, rest_prev, mask):
        z = _dot_nt(qh, k)
        sp = _softplus(z)
        log_keep = -sp if mask is None else jnp.where(mask, -sp, 0.0)
        log_rest = _dot(log_keep.astype(BF16), upper) + rest_prev
        a = jnp.exp(z - sp + log_rest)
        if mask is not None:
            a = jnp.where(mask, a, 0.0)
        return _dot(a.astype(BF16), v), rest_prev + jnp.sum(log_keep, axis=-1, keepdims=True)

    def block(kj, carry, mask):
        r0, r1, acc = carry
        k0 = pl.multiple_of(kj * TK, TK)
        k = k_ref[0, pl.ds(k0, TK), :]
        v = v_ref[0, pl.ds(k0, TK), :]
        pv0, r0 = head(q0, k, v, r0, mask)
        pv1, r1 = head(q1, k, v, r1, mask)
        return r0, r1, acc + jnp.where(lo, pv0, pv1)

    zero = jnp.zeros((TQ, 1), F32)
    carry = block(i, (zero, zero, jnp.zeros((TQ, LANES), F32)), strict)
    carry = lax.fori_loop(0, i, lambda t, c: block(i - 1 - t, c, None), carry)
    o_ref[0] = (carry[2] * _silu(g_ref[0].astype(F32))).astype(BF16)


def _sb_attention(p3):
    b, seq, _ = p3.shape
    gb = GROUP_BLOCKS
    return pl.pallas_call(
        _sb_kernel,
        grid=(b, gb, seq // TQ),
        in_specs=_attn_specs(4 * gb, 5 * gb, 6 * gb, 7 * gb, seq),
        out_specs=pl.BlockSpec((1, TQ, LANES), lambda b, u, i: (b, i, u)),
        out_shape=jax.ShapeDtypeStruct((b, seq, W_GROUP), BF16),
        compiler_params=_ATTN_PARAMS,
        name="sb_attention",
    )(p3, p3, p3, p3)


def _fox_kernel(q_ref, k_ref, v_ref, g_ref, cq_ref, ck_ref, o_ref):
    i = pl.program_id(2)
    q0, q1 = _split_heads(q_ref[0])
    lo = _lane_lo()
    cq = cq_ref[0, 0]
    cq0, cq1 = cq[:, 0:1], cq[:, HEAD_DIM:HEAD_DIM + 1]
    row, col = _tile_iotas()
    causal = col <= row

    def head(qh, cqh, ckh, k, v, state, mask):
        m, l = state
        s = _dot_nt(qh, k) + cqh - ckh
        if mask is not None:
            s = jnp.where(mask, s, NEG_BIG)
        m_new = jnp.maximum(m, jnp.max(s, axis=-1, keepdims=True))
        alpha = jnp.exp(m - m_new)
        p = jnp.exp(s - m_new)
        l = alpha * l + jnp.sum(p, axis=-1, keepdims=True)
        return _dot(p.astype(BF16), v), alpha, (m_new, l)

    def block(kj, carry, mask):
        st0, st1, acc = carry
        k0 = pl.multiple_of(kj * TK, TK)
        k = k_ref[0, pl.ds(k0, TK), :]
        v = v_ref[0, pl.ds(k0, TK), :]
        ck = ck_ref[0, 0, :, pl.ds(k0, TK)]
        pv0, a0, st0 = head(q0, cq0, ck[0:1], k, v, st0, mask)
        pv1, a1, st1 = head(q1, cq1, ck[1:2], k, v, st1, mask)
        return st0, st1, jnp.where(lo, a0, a1) * acc + jnp.where(lo, pv0, pv1)

    init = (jnp.full((TQ, 1), NEG_BIG, F32), jnp.zeros((TQ, 1), F32))
    carry = block(i, (init, init, jnp.zeros((TQ, LANES), F32)), causal)
    st0, st1, acc = lax.fori_loop(0, i, lambda kj, c: block(kj, c, None), carry)
    o = acc / jnp.where(lo, st0[1], st1[1])
    o_ref[0] = (o * _silu(g_ref[0].astype(F32))).astype(BF16)


def _fox_attention(p3, cq, ck):
    b, seq, _ = p3.shape
    gb = GROUP_BLOCKS
    return pl.pallas_call(
        _fox_kernel,
        grid=(b, gb, seq // TQ),
        in_specs=_attn_specs(8 * gb, 9 * gb, 10 * gb, 11 * gb, seq) + [
            pl.BlockSpec((1, 1, TQ, LANES), lambda b, u, i: (b, u, i, 0)),
            pl.BlockSpec((1, 1, 2, seq), lambda b, u, i: (b, u, 0, 0)),
        ],
        out_specs=pl.BlockSpec((1, TQ, LANES), lambda b, u, i: (b, i, u)),
        out_shape=jax.ShapeDtypeStruct((b, seq, W_GROUP), BF16),
        compiler_params=_ATTN_PARAMS,
        name="fox_attention",
    )(p3, p3, p3, p3, cq, ck)


def _out_proj_kernel(ya_ref, yb_ref, yc_ref, w_ref, x_ref, g_ref, o_ref):
    y = (_dot(ya_ref[...], w_ref[0:W_GROUP, :])
         + _dot(yb_ref[...], w_ref[W_GROUP:2 * W_GROUP, :])
         + _dot(yc_ref[...], w_ref[2 * W_GROUP:3 * W_GROUP, :]))
    ms = jnp.mean(y * y, axis=-1, keepdims=True)
    o_ref[...] = x_ref[...] + y * lax.rsqrt(ms + NORM_EPS) * g_ref[...]


def _out_proj(ya, yb, yc, w_out, x2, g_post):
    m = x2.shape[0]
    y_spec = pl.BlockSpec((TM_OUT, W_GROUP), lambda i: (i, 0))
    return pl.pallas_call(
        _out_proj_kernel,
        grid=(m // TM_OUT,),
        in_specs=[
            y_spec, y_spec, y_spec,
            pl.BlockSpec((D_MIX, D_MODEL), lambda i: (0, 0)),
            pl.BlockSpec((TM_OUT, D_MODEL), lambda i: (i, 0)),
            pl.BlockSpec((1, D_MODEL), lambda i: (0, 0)),
        ],
        out_specs=pl.BlockSpec((TM_OUT, D_MODEL), lambda i: (i, 0)),
        out_shape=jax.ShapeDtypeStruct((m, D_MODEL), F32),
        compiler_params=pltpu.CompilerParams(
            dimension_semantics=("arbitrary",), vmem_limit_bytes=VMEM_LIMIT),
        name="out_proj",
    )(ya, yb, yc, w_out, x2, g_post)


def _rope_tables(seq):
    pos = jnp.arange(seq, dtype=F32)
    inv_freq = ROPE_THETA ** (-jnp.arange(0, ROT_DIM, 2, dtype=F32) / ROT_DIM)
    ang = pos[:, None] * inv_freq[None, :]
    cos, sin = jnp.cos(ang), jnp.sin(ang)
    half = ROT_DIM // 2
    ones = jnp.ones((seq, HEAD_DIM - ROT_DIM), F32)
    zeros_rest = jnp.zeros((seq, HEAD_DIM - ROT_DIM), F32)
    zeros_half = jnp.zeros((seq, half), F32)
    rc = jnp.concatenate([cos, cos, ones], axis=-1)
    rs1 = jnp.concatenate([zeros_half, sin, zeros_rest], axis=-1)
    rs2 = jnp.concatenate([-sin, zeros_half, zeros_rest], axis=-1)
    tile = lambda t: jnp.concatenate([t, t], axis=-1)
    return tile(rc), tile(rs1), tile(rs2)


def _layer(x2, batch, seq, w_in, f_bias, lam_vec, subln, w_out, g_pre, g_post, layer_idx, tables):
    scale = HEAD_DIM ** -0.5
    col_group = jnp.arange(D_MAIN) // W_GROUP
    col_scale = jnp.where(col_group % 4 == 0, scale, 1.0).astype(F32)
    w_main = (w_in[:, :D_MAIN] * col_scale).astype(BF16)
    w_f = jnp.pad(w_in[:, D_MAIN:], ((0, 0), (0, LANES - N_HEADS_FOX))).astype(BF16)
    bias = jnp.pad(f_bias, (0, LANES - N_HEADS_FOX)).reshape(1, LANES)

    p, ff = _in_proj(x2, g_pre.reshape(1, D_MODEL), w_main, w_f, *tables, seq)
    p3 = p.reshape(batch, seq, D_MAIN)
    cum = _cumsum(ff.reshape(batch, seq, LANES), bias)[:, :, :N_HEADS_FOX]
    pairs = N_HEADS_FOX // 2
    ck = cum.transpose(0, 2, 1).reshape(batch, pairs, 2, seq)
    cq = jnp.repeat(cum.reshape(batch, seq, pairs, 2).transpose(0, 2, 1, 3), HEAD_DIM, axis=-1)

    lambda_init = 0.8 - 0.6 * math.exp(-0.3 * layer_idx)
    ya = _diff_attention(p3, lam_vec, subln.reshape(1, LANES), lambda_init)
    yb = _sb_attention(p3)
    yc = _fox_attention(p3, cq, ck)

    m = batch * seq
    return _out_proj(ya.reshape(m, W_GROUP), yb.reshape(m, W_GROUP), yc.reshape(m, W_GROUP),
                     w_out.astype(BF16), x2, g_post.reshape(1, D_MODEL))


def kernel(x, w_in, forget_bias, diff_lambda, diff_subln, w_out, pre_norm, post_norm):
    batch, seq, d = x.shape
    tables = _rope_tables(seq)
    x2 = x.reshape(batch * seq, d)
    for l in range(w_in.shape[0]):
        x2 = _layer(x2, batch, seq, w_in[l], forget_bias[l], diff_lambda[l], diff_subln[l], w_out[l],
                    pre_norm[l], post_norm[l], l, tables)
    return x2.reshape(batch, seq, d)
```

```python
import functools
import math

import jax
import jax.numpy as jnp
from jax import lax
from jax.experimental import pallas as pl
from jax.experimental.pallas import tpu as pltpu

F32 = jnp.float32
BF16 = jnp.bfloat16

D_MODEL = 1024
HEAD_DIM = 64
CHUNK = 64
ROT_DIM = HEAD_DIM // 4
ROPE_THETA = 500000.0
NORM_EPS = 1e-6
W_GROUP = 512
N_HEADS_FOX = 8
D_MAIN = 12 * W_GROUP
D_MIX = 3 * W_GROUP

LANES = 128
GROUP_BLOCKS = W_GROUP // LANES
VMEM_LIMIT = 56 * 1024 * 1024

TM_IN = 1024
TN_IN = 1024
TM_OUT = 512
TQ = 512
TK = 512
CS_BLOCK = 256
NEG_BIG = -1e30


def _dot(a, b):
    return jnp.dot(a, b, preferred_element_type=F32)


def _dot_nt(a, b):
    return lax.dot_general(a, b, (((1,), (1,)), ((), ())), preferred_element_type=F32)


def _softplus(z):
    return jnp.maximum(z, 0.0) + jnp.log(1.0 + jnp.exp(-jnp.abs(z)))


def _silu(g):
    return g / (1.0 + jnp.exp(-g))


def _in_proj_kernel(x_ref, g_ref, w_ref, wf_ref, rc_ref, rs1_ref, rs2_ref, p_ref, ff_ref, h_ref):
    j = pl.program_id(1)

    @pl.when(j == 0)
    def _():
        x = x_ref[...]
        ms = jnp.mean(x * x, axis=-1, keepdims=True)
        h = (x * lax.rsqrt(ms + NORM_EPS) * g_ref[...]).astype(BF16)
        h_ref[...] = h
        ff_ref[...] = _dot(h, wf_ref[...])

    acc = _dot(h_ref[...], w_ref[...])

    @pl.when(j == 0)
    def _():
        rc, rs1, rs2 = rc_ref[...], rs1_ref[...], rs2_ref[...]
        for c in range(TN_IN // LANES):
            a = acc[:, c * LANES:(c + 1) * LANES]
            r = a * rc + pltpu.roll(a, ROT_DIM // 2, 1) * rs1 + pltpu.roll(a, LANES - ROT_DIM // 2, 1) * rs2
            p_ref[:, c * LANES:(c + 1) * LANES] = r.astype(BF16)

    @pl.when(j != 0)
    def _():
        p_ref[...] = acc.astype(BF16)


def _in_proj(x2, g_pre, w_main, w_f, rc, rs1, rs2, seq):
    m = x2.shape[0]
    n_pos_tiles = seq // TM_IN
    return pl.pallas_call(
        _in_proj_kernel,
        grid=(m // TM_IN, D_MAIN // TN_IN),
        in_specs=[
            pl.BlockSpec((TM_IN, D_MODEL), lambda i, j: (i, 0)),
            pl.BlockSpec((1, D_MODEL), lambda i, j: (0, 0)),
            pl.BlockSpec((D_MODEL, TN_IN), lambda i, j: (0, j)),
            pl.BlockSpec((D_MODEL, LANES), lambda i, j: (0, 0)),
            pl.BlockSpec((TM_IN, LANES), lambda i, j: (i % n_pos_tiles, 0)),
            pl.BlockSpec((TM_IN, LANES), lambda i, j: (i % n_pos_tiles, 0)),
            pl.BlockSpec((TM_IN, LANES), lambda i, j: (i % n_pos_tiles, 0)),
        ],
        out_specs=[
            pl.BlockSpec((TM_IN, TN_IN), lambda i, j: (i, j)),
            pl.BlockSpec((TM_IN, LANES), lambda i, j: (i, 0)),
        ],
        out_shape=[
            jax.ShapeDtypeStruct((m, D_MAIN), BF16),
            jax.ShapeDtypeStruct((m, LANES), F32),
        ],
        scratch_shapes=[pltpu.VMEM((TM_IN, D_MODEL), BF16)],
        compiler_params=pltpu.CompilerParams(
            dimension_semantics=("arbitrary", "arbitrary"), vmem_limit_bytes=VMEM_LIMIT),
        name="in_proj",
    )(x2, g_pre, w_main, w_f, rc, rs1, rs2)


def _cumsum_kernel(ff_ref, b_ref, o_ref):
    seq = ff_ref.shape[1]
    row = lax.broadcasted_iota(jnp.int32, (CS_BLOCK, CS_BLOCK), 0)
    col = lax.broadcasted_iota(jnp.int32, (CS_BLOCK, CS_BLOCK), 1)
    tri = (col <= row).astype(BF16)

    def body(c, carry):
        r0 = pl.multiple_of(c * CS_BLOCK, CS_BLOCK)
        z = ff_ref[0, pl.ds(r0, CS_BLOCK), :] + b_ref[...]
        lf = -_softplus(-z)
        hi = lf.astype(BF16)
        r1 = lf - hi.astype(F32)
        mid = r1.astype(BF16)
        lo = (r1 - mid.astype(F32)).astype(BF16)
        cs = _dot(tri, hi) + _dot(tri, mid) + _dot(tri, lo) + carry
        o_ref[0, pl.ds(r0, CS_BLOCK), :] = cs
        return cs[CS_BLOCK - 1:CS_BLOCK, :]

    lax.fori_loop(0, seq // CS_BLOCK, body, jnp.zeros((1, LANES), F32))


def _cumsum(ff3, bias):
    b, seq, _ = ff3.shape
    return pl.pallas_call(
        _cumsum_kernel,
        grid=(b,),
        in_specs=[
            pl.BlockSpec((1, seq, LANES), lambda i: (i, 0, 0)),
            pl.BlockSpec((1, LANES), lambda i: (0, 0)),
        ],
        out_specs=pl.BlockSpec((1, seq, LANES), lambda i: (i, 0, 0)),
        out_shape=jax.ShapeDtypeStruct((b, seq, LANES), F32),
        compiler_params=pltpu.CompilerParams(
            dimension_semantics=("arbitrary",), vmem_limit_bytes=VMEM_LIMIT),
        name="cum_log_forget",
    )(ff3, bias)


def _lane_lo():
    return lax.broadcasted_iota(jnp.int32, (1, LANES), 1) < HEAD_DIM


def _split_heads(q):
    lo = _lane_lo()
    zero = jnp.zeros_like(q)
    return jnp.where(lo, q, zero), jnp.where(lo, zero, q)


def _tile_iotas():
    row = lax.broadcasted_iota(jnp.int32, (TQ, TK), 0)
    col = lax.broadcasted_iota(jnp.int32, (TQ, TK), 1)
    return row, col


def _attn_specs(q_blk, k_blk, v_blk, g_blk, seq):
    return [
        pl.BlockSpec((1, TQ, LANES), lambda b, u, i: (b, i, q_blk + u)),
        pl.BlockSpec((1, seq, LANES), lambda b, u, i: (b, 0, k_blk + u)),
        pl.BlockSpec((1, seq, LANES), lambda b, u, i: (b, 0, v_blk + u)),
        pl.BlockSpec((1, TQ, LANES), lambda b, u, i: (b, i, g_blk + u)),
    ]


_ATTN_PARAMS = pltpu.CompilerParams(
    dimension_semantics=("arbitrary", "arbitrary", "arbitrary"), vmem_limit_bytes=VMEM_LIMIT)


def _diff_kernel(q_ref, k_ref, v_ref, g_ref, lam_ref, sub_ref, o_ref, *, lambda_init):
    i = pl.program_id(2)
    q0, q1 = _split_heads(q_ref[0])

    def scores(kj):
        k0 = pl.multiple_of(kj * TK, TK)
        k = k_ref[0, pl.ds(k0, TK), :]
        v = v_ref[0, pl.ds(k0, TK), :]
        return _dot_nt(q0, k), _dot_nt(q1, k), v

    def update(state, s, v):
        m, l, acc = state
        m_new = jnp.maximum(m, jnp.max(s, axis=-1, keepdims=True))
        alpha = jnp.exp(m - m_new)
        p = jnp.exp(s - m_new)
        l = alpha * l + jnp.sum(p, axis=-1, keepdims=True)
        acc = alpha * acc + _dot(p.astype(BF16), v)
        return m_new, l, acc

    row, col = _tile_iotas()
    allowed = (col // CHUNK) <= (row // CHUNK)
    s0, s1, v = scores(i)
    init = (jnp.full((TQ, 1), NEG_BIG, F32), jnp.zeros((TQ, 1), F32), jnp.zeros((TQ, LANES), F32))
    st0 = update(init, jnp.where(allowed, s0, NEG_BIG), v)
    st1 = update(init, jnp.where(allowed, s1, NEG_BIG), v)

    def body(kj, carry):
        st0, st1 = carry
        s0, s1, v = scores(kj)
        return update(st0, s0, v), update(st1, s1, v)

    st0, st1 = lax.fori_loop(0, i, body, (st0, st1))

    lv = lam_ref[...]
    lam = (jnp.exp(jnp.sum(lv[0:1] * lv[1:2], axis=-1, keepdims=True))
           - jnp.exp(jnp.sum(lv[2:3] * lv[3:4], axis=-1, keepdims=True)) + lambda_init)
    o = st0[2] / st0[1] - lam * (st1[2] / st1[1])
    o = o * lax.rsqrt(jnp.mean(o * o, axis=-1, keepdims=True) + NORM_EPS) * sub_ref[...]
    o = o * (1.0 - lambda_init)
    o_ref[0] = (o * _silu(g_ref[0].astype(F32))).astype(BF16)


def _diff_attention(p3, lam_vec, subln, lambda_init):
    b, seq, _ = p3.shape
    gb = GROUP_BLOCKS
    return pl.pallas_call(
        functools.partial(_diff_kernel, lambda_init=lambda_init),
        grid=(b, gb, seq // TQ),
        in_specs=_attn_specs(0 * gb, 1 * gb, 2 * gb, 3 * gb, seq) + [
            pl.BlockSpec((4, HEAD_DIM), lambda b, u, i: (0, 0)),
            pl.BlockSpec((1, LANES), lambda b, u, i: (0, 0)),
        ],
        out_specs=pl.BlockSpec((1, TQ, LANES), lambda b, u, i: (b, i, u)),
        out_shape=jax.ShapeDtypeStruct((b, seq, W_GROUP), BF16),
        compiler_params=_ATTN_PARAMS,
        name="diff_attention",
    )(p3, p3, p3, p3, lam_vec, subln)


def _sb_kernel(q_ref, k_ref, v_ref, g_ref, o_ref):
    i = pl.program_id(2)
    q0, q1 = _split_heads(q_ref[0])
    lo = _lane_lo()
    row, col = _tile_iotas()
    strict = col < row
    upper = (col < row).astype(BF16)

    def head(qh, k, v, rest_prev, mask):
        z = _dot_nt(qh, k)
        sp = _softplus(z)
        log_keep = -sp if mask is None else jnp.where(mask, -sp, 0.0)
        log_rest = _dot(log_keep.astype(BF16), upper) + rest_prev
        a = jnp.exp(z - sp + log_rest)
        if mask is not None:
            a = jnp.where(mask, a, 0.0)
        return _dot(a.astype(BF16), v), rest_prev + jnp.sum(log_keep, axis=-1, keepdims=True)

    def block(kj, carry, mask):
        r0, r1, acc = carry
        k0 = pl.multiple_of(kj * TK, TK)
        k = k_ref[0, pl.ds(k0, TK), :]
        v = v_ref[0, pl.ds(k0, TK), :]
        pv0, r0 = head(q0, k, v, r0, mask)
        pv1, r1 = head(q1, k, v, r1, mask)
        return r0, r1, acc + jnp.where(lo, pv0, pv1)

    zero = jnp.zeros((TQ, 1), F32)
    carry = block(i, (zero, zero, jnp.zeros((TQ, LANES), F32)), strict)
    carry = lax.fori_loop(0, i, lambda t, c: block(i - 1 - t, c, None), carry)
    o_ref[0] = (carry[2] * _silu(g_ref[0].astype(F32))).astype(BF16)


def _sb_attention(p3):
    b, seq, _ = p3.shape
    gb = GROUP_BLOCKS
    return pl.pallas_call(
        _sb_kernel,
        grid=(b, gb, seq // TQ),
        in_specs=_attn_specs(4 * gb, 5 * gb, 6 * gb, 7 * gb, seq),
        out_specs=pl.BlockSpec((1, TQ, LANES), lambda b, u, i: (b, i, u)),
        out_shape=jax.ShapeDtypeStruct((b, seq, W_GROUP), BF16),
        compiler_params=_ATTN_PARAMS,
        name="sb_attention",
    )(p3, p3, p3, p3)


def _fox_kernel(q_ref, k_ref, v_ref, g_ref, cq_ref, ck_ref, o_ref):
    i = pl.program_id(2)
    q0, q1 = _split_heads(q_ref[0])
    lo = _lane_lo()
    cq = cq_ref[0, 0]
    cq0, cq1 = cq[:, 0:1], cq[:, HEAD_DIM:HEAD_DIM + 1]
    row, col = _tile_iotas()
    causal = col <= row

    def head(qh, cqh, ckh, k, v, state, mask):
        m, l = state
        s = _dot_nt(qh, k) + cqh - ckh
        if mask is not None:
            s = jnp.where(mask, s, NEG_BIG)
        m_new = jnp.maximum(m, jnp.max(s, axis=-1, keepdims=True))
        alpha = jnp.exp(m - m_new)
        p = jnp.exp(s - m_new)
        l = alpha * l + jnp.sum(p, axis=-1, keepdims=True)
        return _dot(p.astype(BF16), v), alpha, (m_new, l)

    def block(kj, carry, mask):
        st0, st1, acc = carry
        k0 = pl.multiple_of(kj * TK, TK)
        k = k_ref[0, pl.ds(k0, TK), :]
        v = v_ref[0, pl.ds(k0, TK), :]
        ck = ck_ref[0, 0, :, pl.ds(k0, TK)]
        pv0, a0, st0 = head(q0, cq0, ck[0:1], k, v, st0, mask)
        pv1, a1, st1 = head(q1, cq1, ck[1:2], k, v, st1, mask)
        return st0, st1, jnp.where(lo, a0, a1) * acc + jnp.where(lo, pv0, pv1)

    init = (jnp.full((TQ, 1), NEG_BIG, F32), jnp.zeros((TQ, 1), F32))
    carry = block(i, (init, init, jnp.zeros((TQ, LANES), F32)), causal)
    st0, st1, acc = lax.fori_loop(0, i, lambda kj, c: block(kj, c, None), carry)
    o = acc / jnp.where(lo, st0[1], st1[1])
    o_ref[0] = (o * _silu(g_ref[0].astype(F32))).astype(BF16)


def _fox_attention(p3, cq, ck):
    b, seq, _ = p3.shape
    gb = GROUP_BLOCKS
    return pl.pallas_call(
        _fox_kernel,
        grid=(b, gb, seq // TQ),
        in_specs=_attn_specs(8 * gb, 9 * gb, 10 * gb, 11 * gb, seq) + [
            pl.BlockSpec((1, 1, TQ, LANES), lambda b, u, i: (b, u, i, 0)),
            pl.BlockSpec((1, 1, 2, seq), lambda b, u, i: (b, u, 0, 0)),
        ],
        out_specs=pl.BlockSpec((1, TQ, LANES), lambda b, u, i: (b, i, u)),
        out_shape=jax.ShapeDtypeStruct((b, seq, W_GROUP), BF16),
        compiler_params=_ATTN_PARAMS,
        name="fox_attention",
    )(p3, p3, p3, p3, cq, ck)


def _out_proj_kernel(ya_ref, yb_ref, yc_ref, w_ref, x_ref, g_ref, o_ref):
    y = (_dot(ya_ref[...], w_ref[0:W_GROUP, :])
         + _dot(yb_ref[...], w_ref[W_GROUP:2 * W_GROUP, :])
         + _dot(yc_ref[...], w_ref[2 * W_GROUP:3 * W_GROUP, :]))
    ms = jnp.mean(y * y, axis=-1, keepdims=True)
    o_ref[...] = x_ref[...] + y * lax.rsqrt(ms + NORM_EPS) * g_ref[...]


def _out_proj(ya, yb, yc, w_out, x2, g_post):
    m = x2.shape[0]
    y_spec = pl.BlockSpec((TM_OUT, W_GROUP), lambda i: (i, 0))
    return pl.pallas_call(
        _out_proj_kernel,
        grid=(m // TM_OUT,),
        in_specs=[
            y_spec, y_spec, y_spec,
            pl.BlockSpec((D_MIX, D_MODEL), lambda i: (0, 0)),
            pl.BlockSpec((TM_OUT, D_MODEL), lambda i: (i, 0)),
            pl.BlockSpec((1, D_MODEL), lambda i: (0, 0)),
        ],
        out_specs=pl.BlockSpec((TM_OUT, D_MODEL), lambda i: (i, 0)),
        out_shape=jax.ShapeDtypeStruct((m, D_MODEL), F32),
        compiler_params=pltpu.CompilerParams(
            dimension_semantics=("arbitrary",), vmem_limit_bytes=VMEM_LIMIT),
        name="out_proj",
    )(ya, yb, yc, w_out, x2, g_post)


def _rope_tables(seq):
    pos = jnp.arange(seq, dtype=F32)
    inv_freq = ROPE_THETA ** (-jnp.arange(0, ROT_DIM, 2, dtype=F32) / ROT_DIM)
    ang = pos[:, None] * inv_freq[None, :]
    cos, sin = jnp.cos(ang), jnp.sin(ang)
    half = ROT_DIM // 2
    ones = jnp.ones((seq, HEAD_DIM - ROT_DIM), F32)
    zeros_rest = jnp.zeros((seq, HEAD_DIM - ROT_DIM), F32)
    zeros_half = jnp.zeros((seq, half), F32)
    rc = jnp.concatenate([cos, cos, ones], axis=-1)
    rs1 = jnp.concatenate([zeros_half, sin, zeros_rest], axis=-1)
    rs2 = jnp.concatenate([-sin, zeros_half, zeros_rest], axis=-1)
    tile = lambda t: jnp.concatenate([t, t], axis=-1)
    return tile(rc), tile(rs1), tile(rs2)


def _layer(x2, batch, seq, w_in, f_bias, lam_vec, subln, w_out, g_pre, g_post, layer_idx, tables):
    scale = HEAD_DIM ** -0.5
    col_group = jnp.arange(D_MAIN) // W_GROUP
    col_scale = jnp.where(col_group % 4 == 0, scale, 1.0).astype(F32)
    w_main = (w_in[:, :D_MAIN] * col_scale).astype(BF16)
    w_f = jnp.pad(w_in[:, D_MAIN:], ((0, 0), (0, LANES - N_HEADS_FOX))).astype(BF16)
    bias = jnp.pad(f_bias, (0, LANES - N_HEADS_FOX)).reshape(1, LANES)

    p, ff = _in_proj(x2, g_pre.reshape(1, D_MODEL), w_main, w_f, *tables, seq)
    p3 = p.reshape(batch, seq, D_MAIN)
    cum = _cumsum(ff.reshape(batch, seq, LANES), bias)[:, :, :N_HEADS_FOX]
    pairs = N_HEADS_FOX // 2
    ck = cum.transpose(0, 2, 1).reshape(batch, pairs, 2, seq)
    cq = jnp.repeat(cum.reshape(batch, seq, pairs, 2).transpose(0, 2, 1, 3), HEAD_DIM, axis=-1)

    lambda_init = 0.8 - 0.6 * math.exp(-0.3 * layer_idx)
    ya = _diff_attention(p3, lam_vec, subln.reshape(1, LANES), lambda_init)
    yb = _sb_attention(p3)
    yc = _fox_attention(p3, cq, ck)

    m = batch * seq
    return _out_proj(ya.reshape(m, W_GROUP), yb.reshape(m, W_GROUP), yc.reshape(m, W_GROUP),
                     w_out.astype(BF16), x2, g_post.reshape(1, D_MODEL))


def kernel(x, w_in, forget_bias, diff_lambda, diff_subln, w_out, pre_norm, post_norm):
    batch, seq, d = x.shape
    tables = _rope_tables(seq)
    x2 = x.reshape(batch * seq, d)
    for l in range(w_in.shape[0]):
        x2 = _layer(x2, batch, seq, w_in[l], forget_bias[l], diff_lambda[l], diff_subln[l], w_out[l],
                    pre_norm[l], post_norm[l], l, tables)
    return x2.reshape(batch, seq, d)
```

```python
import functools
import math

import jax
import jax.numpy as jnp
from jax import lax
from jax.experimental import pallas as pl
from jax.experimental.pallas import tpu as pltpu

F32 = jnp.float32
BF16 = jnp.bfloat16

D_MODEL = 1024
HEAD_DIM = 64
CHUNK = 64
ROT_DIM = HEAD_DIM // 4
ROPE_THETA = 500000.0
NORM_EPS = 1e-6
W_GROUP = 512
N_HEADS_FOX = 8
D_MAIN = 12 * W_GROUP
D_MIX = 3 * W_GROUP
LOG2E = math.log2(math.e)

LANES = 128
GROUP_BLOCKS = W_GROUP // LANES
VMEM_LIMIT = 56 * 1024 * 1024

TM_IN = 2048
TN_IN = 1024
TM_OUT = 1024
TQ = 1024
TK = 512
CS_BLOCK = 256
SB_SEG = 256
NEG_BIG = -1e30


def _dot(a, b):
    return jnp.dot(a, b, preferred_element_type=F32)


def _dot_nt(a, b):
    return lax.dot_general(a, b, (((1,), (1,)), ((), ())), preferred_element_type=F32)


def _softplus(z):
    return jnp.maximum(z, 0.0) + jnp.log(1.0 + jnp.exp(-jnp.abs(z)))


def _silu(g):
    return g / (1.0 + jnp.exp(-g))


def _in_proj_kernel(x_ref, g_ref, w_ref, wf_ref, rc_ref, rs1_ref, rs2_ref, p_ref, ff_ref, h_ref):
    j = pl.program_id(1)

    @pl.when(j == 0)
    def _():
        x = x_ref[...]
        ms = jnp.mean(x * x, axis=-1, keepdims=True)
        h = (x * lax.rsqrt(ms + NORM_EPS) * g_ref[...]).astype(BF16)
        h_ref[...] = h
        ff_ref[...] = _dot(h, wf_ref[...])

    acc = _dot(h_ref[...], w_ref[...])

    @pl.when(j == 0)
    def _():
        rc, rs1, rs2 = rc_ref[...], rs1_ref[...], rs2_ref[...]
        for c in range(TN_IN // LANES):
            a = acc[:, c * LANES:(c + 1) * LANES]
            r = a * rc + pltpu.roll(a, ROT_DIM // 2, 1) * rs1 + pltpu.roll(a, LANES - ROT_DIM // 2, 1) * rs2
            p_ref[:, c * LANES:(c + 1) * LANES] = r.astype(BF16)

    @pl.when(j != 0)
    def _():
        p_ref[...] = acc.astype(BF16)


def _in_proj(x2, g_pre, w_main, w_f, rc, rs1, rs2, seq):
    m = x2.shape[0]
    n_pos_tiles = seq // TM_IN
    return pl.pallas_call(
        _in_proj_kernel,
        grid=(m // TM_IN, D_MAIN // TN_IN),
        in_specs=[
            pl.BlockSpec((TM_IN, D_MODEL), lambda i, j: (i, 0)),
            pl.BlockSpec((1, D_MODEL), lambda i, j: (0, 0)),
            pl.BlockSpec((D_MODEL, TN_IN), lambda i, j: (0, j)),
            pl.BlockSpec((D_MODEL, LANES), lambda i, j: (0, 0)),
            pl.BlockSpec((TM_IN, LANES), lambda i, j: (i % n_pos_tiles, 0)),
            pl.BlockSpec((TM_IN, LANES), lambda i, j: (i % n_pos_tiles, 0)),
            pl.BlockSpec((TM_IN, LANES), lambda i, j: (i % n_pos_tiles, 0)),
        ],
        out_specs=[
            pl.BlockSpec((TM_IN, TN_IN), lambda i, j: (i, j)),
            pl.BlockSpec((TM_IN, LANES), lambda i, j: (i, 0)),
        ],
        out_shape=[
            jax.ShapeDtypeStruct((m, D_MAIN), BF16),
            jax.ShapeDtypeStruct((m, LANES), F32),
        ],
        scratch_shapes=[pltpu.VMEM((TM_IN, D_MODEL), BF16)],
        compiler_params=pltpu.CompilerParams(
            dimension_semantics=("arbitrary", "arbitrary"), vmem_limit_bytes=VMEM_LIMIT),
        name="in_proj",
    )(x2, g_pre, w_main, w_f, rc, rs1, rs2)


def _cumsum_kernel(ff_ref, b_ref, o_ref):
    seq = ff_ref.shape[1]
    row = lax.broadcasted_iota(jnp.int32, (CS_BLOCK, CS_BLOCK), 0)
    col = lax.broadcasted_iota(jnp.int32, (CS_BLOCK, CS_BLOCK), 1)
    tri = (col <= row).astype(BF16)

    def body(c, carry):
        r0 = pl.multiple_of(c * CS_BLOCK, CS_BLOCK)
        z = ff_ref[0, pl.ds(r0, CS_BLOCK), :] + b_ref[...]
        lf = -_softplus(-z) * LOG2E
        hi = lf.astype(BF16)
        r1 = lf - hi.astype(F32)
        mid = r1.astype(BF16)
        lo = (r1 - mid.astype(F32)).astype(BF16)
        cs = _dot(tri, hi) + _dot(tri, mid) + _dot(tri, lo) + carry
        o_ref[0, pl.ds(r0, CS_BLOCK), :] = cs
        return cs[CS_BLOCK - 1:CS_BLOCK, :]

    lax.fori_loop(0, seq // CS_BLOCK, body, jnp.zeros((1, LANES), F32))


def _cumsum(ff3, bias):
    b, seq, _ = ff3.shape
    return pl.pallas_call(
        _cumsum_kernel,
        grid=(b,),
        in_specs=[
            pl.BlockSpec((1, seq, LANES), lambda i: (i, 0, 0)),
            pl.BlockSpec((1, LANES), lambda i: (0, 0)),
        ],
        out_specs=pl.BlockSpec((1, seq, LANES), lambda i: (i, 0, 0)),
        out_shape=jax.ShapeDtypeStruct((b, seq, LANES), F32),
        compiler_params=pltpu.CompilerParams(
            dimension_semantics=("arbitrary",), vmem_limit_bytes=VMEM_LIMIT),
        name="cum_log_forget",
    )(ff3, bias)


def _lane_lo():
    return lax.broadcasted_iota(jnp.int32, (1, LANES), 1) < HEAD_DIM


def _split_heads(q):
    lo = _lane_lo()
    zero = jnp.zeros_like(q)
    return jnp.where(lo, q, zero), jnp.where(lo, zero, q)


def _fill_v_transposed(v_ref, vt_ref):
    @pl.when(pl.program_id(2) == 0)
    def _():
        for c in range(vt_ref.shape[0]):
            vt_ref[c] = v_ref[0, c * TK:(c + 1) * TK, :].astype(F32).T.astype(BF16)


def _softmax_step(state, s, vt):
    m, l, acc = state
    m_new = jnp.maximum(m, jnp.max(s, axis=0, keepdims=True))
    alpha = jnp.exp2(m - m_new)
    p = jnp.exp2(s - m_new)
    l = alpha * l + jnp.sum(p, axis=0, keepdims=True)
    acc = alpha * acc + _dot(vt, p.astype(BF16))
    return m_new, l, acc


def _diag_update(state, d, kj, scores, update, visible):
    first = d * TK
    key = lax.broadcasted_iota(jnp.int32, (TK, TQ - first), 0)
    qry = lax.broadcasted_iota(jnp.int32, (TK, TQ - first), 1)
    lanes = slice(first, TQ)
    touched = jax.tree.map(lambda a: a[:, lanes], state)
    touched = update(touched, scores(kj, lanes), kj, visible(key, qry))
    if first == 0:
        return touched
    return jax.tree.map(lambda a, b: jnp.concatenate([a[:, :first], b], axis=1), state, touched)


def _sweep_diagonal_last(i, scores, update, state, visible):
    per_trip = TQ // TK
    n_off = per_trip * i

    def body(t, st):
        for d in range(per_trip):
            st = update(st, scores(per_trip * t + d, slice(None)), per_trip * t + d, None)
        return st

    state = lax.fori_loop(0, i, body, state)
    for d in range(per_trip):
        state = _diag_update(state, d, n_off + d, scores, update, visible)
    return state


def _softmax_init(d):
    return (jnp.full((1, TQ), NEG_BIG, F32), jnp.zeros((1, TQ), F32), jnp.zeros((d, TQ), F32))


def _attn_specs(q_blk, k_blk, v_blk, g_blk, seq):
    return [
        pl.BlockSpec((1, TQ, LANES), lambda b, u, i: (b, i, q_blk + u)),
        pl.BlockSpec((1, seq, LANES), lambda b, u, i: (b, 0, k_blk + u)),
        pl.BlockSpec((1, seq, LANES), lambda b, u, i: (b, 0, v_blk + u)),
        pl.BlockSpec((1, TQ, LANES), lambda b, u, i: (b, i, g_blk + u)),
    ]


def _vt_scratch(seq):
    return pltpu.VMEM((seq // TK, LANES, TK), BF16)


_ATTN_PARAMS = pltpu.CompilerParams(
    dimension_semantics=("arbitrary", "arbitrary", "arbitrary"), vmem_limit_bytes=VMEM_LIMIT)


def _diff_kernel(q_ref, k_ref, v_ref, g_ref, lam_ref, sub_ref, o_ref, vt_ref, *, lambda_init):
    i = pl.program_id(2)
    _fill_v_transposed(v_ref, vt_ref)
    q0, q1 = _split_heads(q_ref[0])

    def scores(kj, queries):
        k0 = pl.multiple_of(kj * TK, TK)
        k = k_ref[0, pl.ds(k0, TK), :]
        return _dot_nt(k, q0[queries]), _dot_nt(k, q1[queries])

    def update(state, s, kj, mask):
        vt = vt_ref[kj]
        s0, s1 = s
        if mask is not None:
            s0, s1 = jnp.where(mask, s0, NEG_BIG), jnp.where(mask, s1, NEG_BIG)
        return _softmax_step(state[0], s0, vt), _softmax_step(state[1], s1, vt)

    visible = lambda key, qry: (key // CHUNK) <= (qry // CHUNK)
    init = (_softmax_init(LANES), _softmax_init(LANES))
    st0, st1 = _sweep_diagonal_last(i, scores, update, init, visible)

    lv = lam_ref[...]
    lam = (jnp.exp(jnp.sum(lv[0:1] * lv[1:2], axis=-1, keepdims=True))
           - jnp.exp(jnp.sum(lv[2:3] * lv[3:4], axis=-1, keepdims=True)) + lambda_init)
    o = (st0[2] / st0[1] - lam * (st1[2] / st1[1])).T
    o = o * lax.rsqrt(jnp.mean(o * o, axis=-1, keepdims=True) + NORM_EPS) * sub_ref[...]
    o = o * (1.0 - lambda_init)
    o_ref[0] = (o * _silu(g_ref[0].astype(F32))).astype(BF16)


def _diff_attention(p3, lam_vec, subln, lambda_init):
    b, seq, _ = p3.shape
    gb = GROUP_BLOCKS
    return pl.pallas_call(
        functools.partial(_diff_kernel, lambda_init=lambda_init),
        grid=(b, gb, seq // TQ),
        in_specs=_attn_specs(0 * gb, 1 * gb, 2 * gb, 3 * gb, seq) + [
            pl.BlockSpec((4, HEAD_DIM), lambda b, u, i: (0, 0)),
            pl.BlockSpec((1, LANES), lambda b, u, i: (0, 0)),
        ],
        out_specs=pl.BlockSpec((1, TQ, LANES), lambda b, u, i: (b, i, u)),
        out_shape=jax.ShapeDtypeStruct((b, seq, W_GROUP), BF16),
        scratch_shapes=[_vt_scratch(seq)],
        compiler_params=_ATTN_PARAMS,
        name="diff_attention",
    )(p3, p3, p3, p3, lam_vec, subln)


def _sb_kernel(q_ref, k_ref, v_ref, g_ref, o_ref, vt_ref):
    i = pl.program_id(2)
    _fill_v_transposed(v_ref, vt_ref)
    q0, q1 = _split_heads(q_ref[0])
    row = lax.broadcasted_iota(jnp.int32, (SB_SEG, SB_SEG), 0)
    col = lax.broadcasted_iota(jnp.int32, (SB_SEG, SB_SEG), 1)
    after = (col > row).astype(BF16)
    def head(y, vt, state, mask):
        rest_prev, acc = state
        neg_part = jnp.minimum(y, 0.0)
        neg_abs = neg_part + (neg_part - y)
        log_keep = neg_part - jnp.log2(1.0 + jnp.exp2(neg_abs))
        if mask is not None:
            log_keep = jnp.where(mask, log_keep, 0.0)
        lkb = log_keep.astype(BF16)
        rests, later = [], rest_prev
        for g in reversed(range(TK // SB_SEG)):
            rows = slice(g * SB_SEG, (g + 1) * SB_SEG)
            suffix = _dot(after, lkb[rows])
            rests.insert(0, suffix + later)
            later = later + (suffix[0:1] + log_keep[g * SB_SEG:g * SB_SEG + 1])
        rest = rests[0] if len(rests) == 1 else jnp.concatenate(rests, axis=0)
        a = jnp.exp2(log_keep - y + rest)
        if mask is not None:
            a = jnp.where(mask, a, 0.0)
        return later, acc + _dot(vt, a.astype(BF16))

    def scores(kj, queries):
        k0 = pl.multiple_of(kj * TK, TK)
        k = k_ref[0, pl.ds(k0, TK), :]
        return _dot_nt(k, q0[queries]), _dot_nt(k, q1[queries])

    def update(state, y, kj, mask):
        vt = vt_ref[kj]
        return head(y[0], vt[:HEAD_DIM], state[0], mask), head(y[1], vt[HEAD_DIM:], state[1], mask)

    init = (jnp.zeros((1, TQ), F32), jnp.zeros((HEAD_DIM, TQ), F32))
    state = (init, init)
    per_trip = TQ // TK
    n_off = per_trip * i
    for d in reversed(range(per_trip)):
        state = _diag_update(state, d, n_off + d, scores, update, lambda key, qry: key < qry)

    def body(t, st):
        for d in range(per_trip):
            kj = n_off - 1 - (per_trip * t + d)
            st = update(st, scores(kj, slice(None)), kj, None)
        return st

    st0, st1 = lax.fori_loop(0, i, body, state)
    o = jnp.concatenate([st0[1], st1[1]], axis=0).T
    o_ref[0] = (o * _silu(g_ref[0].astype(F32))).astype(BF16)


def _sb_attention(p3):
    b, seq, _ = p3.shape
    gb = GROUP_BLOCKS
    return pl.pallas_call(
        _sb_kernel,
        grid=(b, gb, seq // TQ),
        in_specs=_attn_specs(4 * gb, 5 * gb, 6 * gb, 7 * gb, seq),
        out_specs=pl.BlockSpec((1, TQ, LANES), lambda b, u, i: (b, i, u)),
        out_shape=jax.ShapeDtypeStruct((b, seq, W_GROUP), BF16),
        scratch_shapes=[_vt_scratch(seq)],
        compiler_params=_ATTN_PARAMS,
        name="sb_attention",
    )(p3, p3, p3, p3)


def _fox_kernel(q_ref, k_ref, v_ref, g_ref, cq_ref, ck_ref, o_ref, vt_ref):
    i = pl.program_id(2)
    _fill_v_transposed(v_ref, vt_ref)
    q0, q1 = _split_heads(q_ref[0])
    cq = cq_ref[0, 0]

    def scores(kj, queries):
        k0 = pl.multiple_of(kj * TK, TK)
        k = k_ref[0, pl.ds(k0, TK), :]
        ck = ck_ref[0, 0, pl.ds(k0, TK), :]
        return (_dot_nt(k, q0[queries]) + cq[0:1, queries] - ck[:, 0:1],
                _dot_nt(k, q1[queries]) + cq[1:2, queries] - ck[:, HEAD_DIM:HEAD_DIM + 1])

    def update(state, s, kj, mask):
        vt = vt_ref[kj]
        s0, s1 = s
        if mask is not None:
            s0, s1 = jnp.where(mask, s0, NEG_BIG), jnp.where(mask, s1, NEG_BIG)
        return _softmax_step(state[0], s0, vt[:HEAD_DIM]), _softmax_step(state[1], s1, vt[HEAD_DIM:])

    init = (_softmax_init(HEAD_DIM), _softmax_init(HEAD_DIM))
    st0, st1 = _sweep_diagonal_last(i, scores, update, init, lambda key, qry: key <= qry)
    o = jnp.concatenate([st0[2] / st0[1], st1[2] / st1[1]], axis=0).T
    o_ref[0] = (o * _silu(g_ref[0].astype(F32))).astype(BF16)


def _fox_attention(p3, cq, ck):
    b, seq, _ = p3.shape
    gb = GROUP_BLOCKS
    return pl.pallas_call(
        _fox_kernel,
        grid=(b, gb, seq // TQ),
        in_specs=_attn_specs(8 * gb, 9 * gb, 10 * gb, 11 * gb, seq) + [
            pl.BlockSpec((1, 1, 2, TQ), lambda b, u, i: (b, u, 0, i)),
            pl.BlockSpec((1, 1, seq, LANES), lambda b, u, i: (b, u, 0, 0)),
        ],
        out_specs=pl.BlockSpec((1, TQ, LANES), lambda b, u, i: (b, i, u)),
        out_shape=jax.ShapeDtypeStruct((b, seq, W_GROUP), BF16),
        scratch_shapes=[_vt_scratch(seq)],
        compiler_params=_ATTN_PARAMS,
        name="fox_attention",
    )(p3, p3, p3, p3, cq, ck)


def _out_proj_kernel(ya_ref, yb_ref, yc_ref, w_ref, x_ref, g_ref, o_ref):
    y = (_dot(ya_ref[...], w_ref[0:W_GROUP, :])
         + _dot(yb_ref[...], w_ref[W_GROUP:2 * W_GROUP, :])
         + _dot(yc_ref[...], w_ref[2 * W_GROUP:3 * W_GROUP, :]))
    ms = jnp.mean(y * y, axis=-1, keepdims=True)
    o_ref[...] = x_ref[...] + y * lax.rsqrt(ms + NORM_EPS) * g_ref[...]


def _out_proj(ya, yb, yc, w_out, x2, g_post):
    m = x2.shape[0]
    y_spec = pl.BlockSpec((TM_OUT, W_GROUP), lambda i: (i, 0))
    return pl.pallas_call(
        _out_proj_kernel,
        grid=(m // TM_OUT,),
        in_specs=[
            y_spec, y_spec, y_spec,
            pl.BlockSpec((D_MIX, D_MODEL), lambda i: (0, 0)),
            pl.BlockSpec((TM_OUT, D_MODEL), lambda i: (i, 0)),
            pl.BlockSpec((1, D_MODEL), lambda i: (0, 0)),
        ],
        out_specs=pl.BlockSpec((TM_OUT, D_MODEL), lambda i: (i, 0)),
        out_shape=jax.ShapeDtypeStruct((m, D_MODEL), F32),
        compiler_params=pltpu.CompilerParams(
            dimension_semantics=("arbitrary",), vmem_limit_bytes=VMEM_LIMIT),
        name="out_proj",
    )(ya, yb, yc, w_out, x2, g_post)


def _rope_tables(seq):
    pos = jnp.arange(seq, dtype=F32)
    inv_freq = ROPE_THETA ** (-jnp.arange(0, ROT_DIM, 2, dtype=F32) / ROT_DIM)
    ang = pos[:, None] * inv_freq[None, :]
    cos, sin = jnp.cos(ang), jnp.sin(ang)
    half = ROT_DIM // 2
    ones = jnp.ones((seq, HEAD_DIM - ROT_DIM), F32)
    zeros_rest = jnp.zeros((seq, HEAD_DIM - ROT_DIM), F32)
    zeros_half = jnp.zeros((seq, half), F32)
    rc = jnp.concatenate([cos, cos, ones], axis=-1)
    rs1 = jnp.concatenate([zeros_half, sin, zeros_rest], axis=-1)
    rs2 = jnp.concatenate([-sin, zeros_half, zeros_rest], axis=-1)
    tile = lambda t: jnp.concatenate([t, t], axis=-1)
    return tile(rc), tile(rs1), tile(rs2)


def _q_column_scale():
    scale = LOG2E * HEAD_DIM ** -0.5
    per_group = [1.0] * 12
    per_group[0], per_group[4], per_group[8] = scale, -scale, scale
    return jnp.repeat(jnp.asarray(per_group, F32), W_GROUP)


def _layer(x2, batch, seq, w_main, w_f, f_bias, lam_vec, subln, w_out, g_pre, g_post, layer_idx, tables):
    bias = jnp.pad(f_bias, (0, LANES - N_HEADS_FOX)).reshape(1, LANES)

    p, ff = _in_proj(x2, g_pre.reshape(1, D_MODEL), w_main, w_f, *tables, seq)
    p3 = p.reshape(batch, seq, D_MAIN)
    cum = _cumsum(ff.reshape(batch, seq, LANES), bias)[:, :, :N_HEADS_FOX]
    pairs = N_HEADS_FOX // 2
    cq = cum.transpose(0, 2, 1).reshape(batch, pairs, 2, seq)
    ck = jnp.repeat(cum.reshape(batch, seq, pairs, 2).transpose(0, 2, 1, 3), HEAD_DIM, axis=-1)

    lambda_init = 0.8 - 0.6 * math.exp(-0.3 * layer_idx)
    ya = _diff_attention(p3, lam_vec, subln.reshape(1, LANES), lambda_init)
    yb = _sb_attention(p3)
    yc = _fox_attention(p3, cq, ck)

    m = batch * seq
    return _out_proj(ya.reshape(m, W_GROUP), yb.reshape(m, W_GROUP), yc.reshape(m, W_GROUP),
                     w_out, x2, g_post.reshape(1, D_MODEL))


def kernel(x, w_in, forget_bias, diff_lambda, diff_subln, w_out, pre_norm, post_norm):
    batch, seq, d = x.shape
    tables = _rope_tables(seq)
    w_main = (w_in[:, :, :D_MAIN] * _q_column_scale()).astype(BF16)
    w_f = jnp.pad(w_in[:, :, D_MAIN:], ((0, 0), (0, 0), (0, LANES - N_HEADS_FOX))).astype(BF16)
    w_out = w_out.astype(BF16)
    x2 = x.reshape(batch * seq, d)
    for l in range(w_in.shape[0]):
        x2 = _layer(x2, batch, seq, w_main[l], w_f[l], forget_bias[l], diff_lambda[l], diff_subln[l],
                    w_out[l], pre_norm[l], post_norm[l], l, tables)
    return x2.reshape(batch, seq, d)
```

```python
import functools
import math

import jax
import jax.numpy as jnp
from jax import lax
from jax.experimental import pallas as pl
from jax.experimental.pallas import tpu as pltpu

F32 = jnp.float32
BF16 = jnp.bfloat16

D_MODEL = 1024
HEAD_DIM = 64
CHUNK = 64
ROT_DIM = HEAD_DIM // 4
ROPE_THETA = 500000.0
NORM_EPS = 1e-6
W_GROUP = 512
N_HEADS_FOX = 8
D_MAIN = 12 * W_GROUP
D_MIX = 3 * W_GROUP
LOG2E = math.log2(math.e)

LANES = 128
GROUP_BLOCKS = W_GROUP // LANES
VMEM_LIMIT = 56 * 1024 * 1024

TM_IN = 2048
TN_IN = 1024
TM_OUT = 1024
TQ = 1024
TK = 512
TK_SB = 256
CS_BLOCK = 256
ONES_ROWS = 16
SB_SEG = 256
SB_DEAD = 160.0
NEG_BIG = -1e30


def _dot(a, b):
    return jnp.dot(a, b, preferred_element_type=F32)


def _dot_nt(a, b):
    return lax.dot_general(a, b, (((1,), (1,)), ((), ())), preferred_element_type=F32)


def _softplus(z):
    return jnp.maximum(z, 0.0) + jnp.log(1.0 + jnp.exp(-jnp.abs(z)))


def _silu(g):
    return g / (1.0 + jnp.exp(-g))


def _in_proj_kernel(x_ref, g_ref, w_ref, wf_ref, rc_ref, rs1_ref, rs2_ref, p_ref, ff_ref, h_ref):
    j = pl.program_id(1)

    @pl.when(j == 0)
    def _():
        x = x_ref[...]
        ms = jnp.mean(x * x, axis=-1, keepdims=True)
        h = (x * lax.rsqrt(ms + NORM_EPS) * g_ref[...]).astype(BF16)
        h_ref[...] = h
        ff_ref[...] = _dot(h, wf_ref[...])
        rc, rs1, rs2 = rc_ref[...], rs1_ref[...], rs2_ref[...]
        acc = _dot(h, w_ref[...])
        for c in range(TN_IN // LANES):
            a = acc[:, c * LANES:(c + 1) * LANES]
            r = a * rc + pltpu.roll(a, ROT_DIM // 2, 1) * rs1 + pltpu.roll(a, LANES - ROT_DIM // 2, 1) * rs2
            p_ref[:, c * LANES:(c + 1) * LANES] = r.astype(BF16)

    @pl.when(j != 0)
    def _():
        p_ref[...] = _dot(h_ref[...], w_ref[...]).astype(BF16)


def _in_proj(x2, g_pre, w_all, layer, w_f, rc, rs1, rs2, seq):
    m = x2.shape[0]
    n_pos_tiles = seq // TM_IN
    return pl.pallas_call(
        _in_proj_kernel,
        grid=(m // TM_IN, D_MAIN // TN_IN),
        in_specs=[
            pl.BlockSpec((TM_IN, D_MODEL), lambda i, j: (i, 0)),
            pl.BlockSpec((1, D_MODEL), lambda i, j: (0, 0)),
            pl.BlockSpec((None, D_MODEL, TN_IN), lambda i, j: (layer, 0, j)),
            pl.BlockSpec((D_MODEL, LANES), lambda i, j: (0, 0)),
            pl.BlockSpec((TM_IN, LANES), lambda i, j: (i % n_pos_tiles, 0)),
            pl.BlockSpec((TM_IN, LANES), lambda i, j: (i % n_pos_tiles, 0)),
            pl.BlockSpec((TM_IN, LANES), lambda i, j: (i % n_pos_tiles, 0)),
        ],
        out_specs=[
            pl.BlockSpec((TM_IN, TN_IN), lambda i, j: (i, j)),
            pl.BlockSpec((TM_IN, LANES), lambda i, j: (i, 0)),
        ],
        out_shape=[
            jax.ShapeDtypeStruct((m, D_MAIN), BF16),
            jax.ShapeDtypeStruct((m, LANES), F32),
        ],
        scratch_shapes=[pltpu.VMEM((TM_IN, D_MODEL), BF16)],
        compiler_params=pltpu.CompilerParams(
            dimension_semantics=("arbitrary", "arbitrary"), vmem_limit_bytes=VMEM_LIMIT),
        name="in_proj",
    )(x2, g_pre, w_all, w_f, rc, rs1, rs2)


def _cumsum_kernel(ff_ref, b_ref, o_ref):
    seq = ff_ref.shape[1]
    row = lax.broadcasted_iota(jnp.int32, (CS_BLOCK, CS_BLOCK), 0)
    col = lax.broadcasted_iota(jnp.int32, (CS_BLOCK, CS_BLOCK), 1)
    tri = (col <= row).astype(BF16)

    def body(c, carry):
        r0 = pl.multiple_of(c * CS_BLOCK, CS_BLOCK)
        z = ff_ref[0, pl.ds(r0, CS_BLOCK), :] + b_ref[...]
        lf = -_softplus(-z) * LOG2E
        hi = lf.astype(BF16)
        r1 = lf - hi.astype(F32)
        mid = r1.astype(BF16)
        lo = (r1 - mid.astype(F32)).astype(BF16)
        cs = _dot(tri, hi) + _dot(tri, mid) + _dot(tri, lo) + carry
        o_ref[0, pl.ds(r0, CS_BLOCK), :] = cs
        return cs[CS_BLOCK - 1:CS_BLOCK, :]

    lax.fori_loop(0, seq // CS_BLOCK, body, jnp.zeros((1, LANES), F32))


def _cumsum(ff3, bias):
    b, seq, _ = ff3.shape
    return pl.pallas_call(
        _cumsum_kernel,
        grid=(b,),
        in_specs=[
            pl.BlockSpec((1, seq, LANES), lambda i: (i, 0, 0)),
            pl.BlockSpec((1, LANES), lambda i: (0, 0)),
        ],
        out_specs=pl.BlockSpec((1, seq, LANES), lambda i: (i, 0, 0)),
        out_shape=jax.ShapeDtypeStruct((b, seq, LANES), F32),
        compiler_params=pltpu.CompilerParams(
            dimension_semantics=("arbitrary",), vmem_limit_bytes=VMEM_LIMIT),
        name="cum_log_forget",
    )(ff3, bias)


def _lane_lo():
    return lax.broadcasted_iota(jnp.int32, (1, LANES), 1) < HEAD_DIM


def _split_heads(q):
    lo = _lane_lo()
    zero = jnp.zeros_like(q)
    return jnp.where(lo, q, zero), jnp.where(lo, zero, q)


def _fill_v_transposed(v_ref, vt_ref, head_rows=LANES):
    tk = vt_ref.shape[2]

    @pl.when(pl.program_id(2) == 0)
    def _():
        for c in range(vt_ref.shape[0]):
            vt = v_ref[0, c * tk:(c + 1) * tk, :].astype(F32).T.astype(BF16)
            if vt_ref.shape[1] == LANES:
                vt_ref[c] = vt
                continue
            stride = head_rows + ONES_ROWS
            for h in range(LANES // head_rows):
                vt_ref[c, h * stride:h * stride + head_rows, :] = vt[h * head_rows:(h + 1) * head_rows]
                vt_ref[c, h * stride + head_rows:(h + 1) * stride, :] = jnp.ones((ONES_ROWS, tk), BF16)


def _softmax_step(state, s, vt):
    m, acc = state
    m_new = jnp.maximum(m, jnp.max(s, axis=0, keepdims=True))
    alpha = jnp.exp2(m - m_new)
    p = jnp.exp2(s - m_new)
    acc = alpha * acc + _dot(vt, p.astype(BF16))
    return m_new, acc


def _softmax_result(state, d):
    acc = state[1]
    return acc[:d] / acc[d:d + 1]


def _diag_update(state, d, kj, scores, update, visible, tk=TK):
    first = d * tk
    key = lax.broadcasted_iota(jnp.int32, (tk, TQ - first), 0)
    qry = lax.broadcasted_iota(jnp.int32, (tk, TQ - first), 1)
    lanes = slice(first, TQ)
    touched = jax.tree.map(lambda a: a[:, lanes], state)
    touched = update(touched, scores(kj, lanes), kj, visible(key, qry))
    if first == 0:
        return touched
    return jax.tree.map(lambda a, b: jnp.concatenate([a[:, :first], b], axis=1), state, touched)


def _sweep_diagonal_last(i, scores, update, state, visible):
    per_trip = TQ // TK
    n_off = per_trip * i

    def body(t, st):
        for d in range(per_trip):
            st = update(st, scores(per_trip * t + d, slice(None)), per_trip * t + d, None)
        return st

    state = lax.fori_loop(0, i, body, state)
    for d in range(per_trip):
        state = _diag_update(state, d, n_off + d, scores, update, visible)
    return state


def _softmax_init(d):
    return (jnp.full((1, TQ), NEG_BIG, F32), jnp.zeros((d + ONES_ROWS, TQ), F32))


def _attn_specs(q_blk, k_blk, v_blk, g_blk, seq):
    return [
        pl.BlockSpec((1, TQ, LANES), lambda b, u, i: (b, i, q_blk + u)),
        pl.BlockSpec((1, seq, LANES), lambda b, u, i: (b, 0, k_blk + u)),
        pl.BlockSpec((1, seq, LANES), lambda b, u, i: (b, 0, v_blk + u)),
        pl.BlockSpec((1, TQ, LANES), lambda b, u, i: (b, i, g_blk + u)),
    ]


def _vt_scratch(seq, tk=TK, head_rows=None):
    rows = LANES if head_rows is None else (LANES // head_rows) * (head_rows + ONES_ROWS)
    return pltpu.VMEM((seq // tk, rows, tk), BF16)


_ATTN_PARAMS = pltpu.CompilerParams(
    dimension_semantics=("arbitrary", "arbitrary", "arbitrary"), vmem_limit_bytes=VMEM_LIMIT)


def _diff_kernel(q_ref, k_ref, v_ref, g_ref, lam_ref, sub_ref, o_ref, vt_ref, *, lambda_init):
    i = pl.program_id(2)
    _fill_v_transposed(v_ref, vt_ref, LANES)
    q0, q1 = _split_heads(q_ref[0])

    def scores(kj, queries):
        k0 = pl.multiple_of(kj * TK, TK)
        k = k_ref[0, pl.ds(k0, TK), :]
        return _dot_nt(k, q0[queries]), _dot_nt(k, q1[queries])

    def update(state, s, kj, mask):
        vt = vt_ref[kj]
        s0, s1 = s
        if mask is not None:
            s0, s1 = jnp.where(mask, s0, NEG_BIG), jnp.where(mask, s1, NEG_BIG)
        return _softmax_step(state[0], s0, vt), _softmax_step(state[1], s1, vt)

    visible = lambda key, qry: (key // CHUNK) <= (qry // CHUNK)
    init = (_softmax_init(LANES), _softmax_init(LANES))
    st0, st1 = _sweep_diagonal_last(i, scores, update, init, visible)

    lv = lam_ref[...]
    lam = (jnp.exp(jnp.sum(lv[0:1] * lv[1:2], axis=-1, keepdims=True))
           - jnp.exp(jnp.sum(lv[2:3] * lv[3:4], axis=-1, keepdims=True)) + lambda_init)
    o = (_softmax_result(st0, LANES) - lam * _softmax_result(st1, LANES)).T
    o = o * lax.rsqrt(jnp.mean(o * o, axis=-1, keepdims=True) + NORM_EPS) * sub_ref[...]
    o = o * (1.0 - lambda_init)
    o_ref[0] = (o * _silu(g_ref[0].astype(F32))).astype(BF16)


def _diff_attention(p3, lam_vec, subln, lambda_init):
    b, seq, _ = p3.shape
    gb = GROUP_BLOCKS
    return pl.pallas_call(
        functools.partial(_diff_kernel, lambda_init=lambda_init),
        grid=(b, gb, seq // TQ),
        in_specs=_attn_specs(0 * gb, 1 * gb, 2 * gb, 3 * gb, seq) + [
            pl.BlockSpec((4, HEAD_DIM), lambda b, u, i: (0, 0)),
            pl.BlockSpec((1, LANES), lambda b, u, i: (0, 0)),
        ],
        out_specs=pl.BlockSpec((1, TQ, LANES), lambda b, u, i: (b, i, u)),
        out_shape=jax.ShapeDtypeStruct((b, seq, W_GROUP), BF16),
        scratch_shapes=[_vt_scratch(seq, TK, LANES)],
        compiler_params=_ATTN_PARAMS,
        name="diff_attention",
    )(p3, p3, p3, p3, lam_vec, subln)


def _sb_kernel(q_ref, k_ref, v_ref, g_ref, o_ref, vt_ref):
    i = pl.program_id(2)
    _fill_v_transposed(v_ref, vt_ref)
    q0, q1 = _split_heads(q_ref[0])
    row = lax.broadcasted_iota(jnp.int32, (SB_SEG, SB_SEG), 0)
    col = lax.broadcasted_iota(jnp.int32, (SB_SEG, SB_SEG), 1)
    after = (col > row).astype(BF16)
    def head(y, vt, state, mask):
        rest_prev, acc = state
        neg_part = jnp.minimum(y, 0.0)
        neg_abs = neg_part + (neg_part - y)
        log_keep = neg_part - jnp.log2(1.0 + jnp.exp2(neg_abs))
        if mask is not None:
            log_keep = jnp.where(mask, log_keep, 0.0)
        lkb = log_keep.astype(BF16)
        rests, later = [], rest_prev
        for g in reversed(range(TK_SB // SB_SEG)):
            rows = slice(g * SB_SEG, (g + 1) * SB_SEG)
            suffix = _dot(after, lkb[rows])
            rests.insert(0, suffix + later)
            later = later + (suffix[0:1] + log_keep[g * SB_SEG:g * SB_SEG + 1])
        rest = rests[0] if len(rests) == 1 else jnp.concatenate(rests, axis=0)
        a = jnp.exp2(log_keep - y + rest)
        if mask is not None:
            a = jnp.where(mask, a, 0.0)
        return later, acc + _dot(vt, a.astype(BF16))

    def scores(kj, queries):
        k0 = pl.multiple_of(kj * TK_SB, TK_SB)
        k = k_ref[0, pl.ds(k0, TK_SB), :]
        return _dot_nt(k, q0[queries]), _dot_nt(k, q1[queries])

    def update(state, y, kj, mask):
        vt = vt_ref[kj]
        return head(y[0], vt[:HEAD_DIM], state[0], mask), head(y[1], vt[HEAD_DIM:], state[1], mask)

    init = (jnp.zeros((1, TQ), F32), jnp.zeros((HEAD_DIM, TQ), F32))
    state = (init, init)
    n_diag = TQ // TK_SB
    n_off = n_diag * i
    for d in reversed(range(n_diag)):
        state = _diag_update(state, d, n_off + d, scores, update, lambda key, qry: key < qry, TK_SB)

    def cond(carry):
        t, st = carry
        reach = jnp.maximum(jnp.max(st[0][0]), jnp.max(st[1][0]))
        return jnp.logical_and(t < n_off, reach > -SB_DEAD)

    def body(carry):
        t, st = carry
        kj = n_off - 1 - t
        return t + 1, update(st, scores(kj, slice(None)), kj, None)

    _, (st0, st1) = lax.while_loop(cond, body, (jnp.int32(0), state))
    o = jnp.concatenate([st0[1], st1[1]], axis=0).T
    o_ref[0] = (o * _silu(g_ref[0].astype(F32))).astype(BF16)


def _sb_attention(p3):
    b, seq, _ = p3.shape
    gb = GROUP_BLOCKS
    return pl.pallas_call(
        _sb_kernel,
        grid=(b, gb, seq // TQ),
        in_specs=_attn_specs(4 * gb, 5 * gb, 6 * gb, 7 * gb, seq),
        out_specs=pl.BlockSpec((1, TQ, LANES), lambda b, u, i: (b, i, u)),
        out_shape=jax.ShapeDtypeStruct((b, seq, W_GROUP), BF16),
        scratch_shapes=[_vt_scratch(seq, TK_SB)],
        compiler_params=_ATTN_PARAMS,
        name="sb_attention",
    )(p3, p3, p3, p3)


def _fox_kernel(q_ref, k_ref, v_ref, g_ref, c_ref, o_ref, vt_ref, kaug_ref):
    i = pl.program_id(2)
    seq = k_ref.shape[1]
    _fill_v_transposed(v_ref, vt_ref, HEAD_DIM)
    lane = lax.broadcasted_iota(jnp.int32, (1, LANES), 1)
    slot = lane % HEAD_DIM

    def pieces(c):
        hi = c.astype(BF16).astype(F32)
        rem = c - hi
        mid = rem.astype(BF16).astype(F32)
        return hi, mid, (rem - mid).astype(BF16).astype(F32)

    @pl.when(i == 0)
    def _():
        for blk in range(seq // TK):
            hi, mid, lo = pieces(c_ref[0, 0, blk * TK:(blk + 1) * TK, :])
            kaug = jnp.where(slot < 3, 1.0, jnp.where(slot == 3, -hi, jnp.where(slot == 4, -mid, jnp.where(slot == 5, -lo, 0.0))))
            kaug_ref[blk * TK:(blk + 1) * TK, :] = kaug.astype(BF16)

    hi, mid, lo = pieces(c_ref[0, 0, pl.ds(pl.multiple_of(i * TQ, TQ), TQ), :])
    qaug = jnp.where(slot == 0, hi, jnp.where(slot == 1, mid, jnp.where(slot == 2, lo, jnp.where(slot < 6, 1.0, 0.0))))
    q0, q1 = _split_heads(q_ref[0])
    qa0, qa1 = _split_heads(qaug.astype(BF16))
    q0, q1 = jnp.concatenate([q0, qa0], axis=1), jnp.concatenate([q1, qa1], axis=1)

    def scores(kj, queries):
        k0 = pl.multiple_of(kj * TK, TK)
        k = jnp.concatenate([k_ref[0, pl.ds(k0, TK), :], kaug_ref[pl.ds(k0, TK), :]], axis=1)
        return _dot_nt(k, q0[queries]), _dot_nt(k, q1[queries])

    def update(state, s, kj, mask):
        vt = vt_ref[kj]
        s0, s1 = s
        if mask is not None:
            s0, s1 = jnp.where(mask, s0, NEG_BIG), jnp.where(mask, s1, NEG_BIG)
        rows = HEAD_DIM + ONES_ROWS
        return _softmax_step(state[0], s0, vt[:rows]), _softmax_step(state[1], s1, vt[rows:])

    init = (_softmax_init(HEAD_DIM), _softmax_init(HEAD_DIM))
    st0, st1 = _sweep_diagonal_last(i, scores, update, init, lambda key, qry: key <= qry)
    o = jnp.concatenate([_softmax_result(st0, HEAD_DIM), _softmax_result(st1, HEAD_DIM)], axis=0).T
    o_ref[0] = (o * _silu(g_ref[0].astype(F32))).astype(BF16)


def _fox_attention(p3, cum_rep):
    b, seq, _ = p3.shape
    gb = GROUP_BLOCKS
    return pl.pallas_call(
        _fox_kernel,
        grid=(b, gb, seq // TQ),
        in_specs=_attn_specs(8 * gb, 9 * gb, 10 * gb, 11 * gb, seq) + [
            pl.BlockSpec((1, 1, seq, LANES), lambda b, u, i: (b, u, 0, 0)),
        ],
        out_specs=pl.BlockSpec((1, TQ, LANES), lambda b, u, i: (b, i, u)),
        out_shape=jax.ShapeDtypeStruct((b, seq, W_GROUP), BF16),
        scratch_shapes=[_vt_scratch(seq, TK, HEAD_DIM), pltpu.VMEM((seq, LANES), BF16)],
        compiler_params=_ATTN_PARAMS,
        name="fox_attention",
    )(p3, p3, p3, p3, cum_rep)


def _out_proj_kernel(ya_ref, yb_ref, yc_ref, w_ref, x_ref, g_ref, o_ref):
    y = (_dot(ya_ref[...], w_ref[0:W_GROUP, :])
         + _dot(yb_ref[...], w_ref[W_GROUP:2 * W_GROUP, :])
         + _dot(yc_ref[...], w_ref[2 * W_GROUP:3 * W_GROUP, :]))
    ms = jnp.mean(y * y, axis=-1, keepdims=True)
    o_ref[...] = x_ref[...] + y * lax.rsqrt(ms + NORM_EPS) * g_ref[...]


def _out_proj(ya, yb, yc, w_out, x2, g_post):
    m = x2.shape[0]
    y_spec = pl.BlockSpec((TM_OUT, W_GROUP), lambda i: (i, 0))
    return pl.pallas_call(
        _out_proj_kernel,
        grid=(m // TM_OUT,),
        in_specs=[
            y_spec, y_spec, y_spec,
            pl.BlockSpec((D_MIX, D_MODEL), lambda i: (0, 0)),
            pl.BlockSpec((TM_OUT, D_MODEL), lambda i: (i, 0)),
            pl.BlockSpec((1, D_MODEL), lambda i: (0, 0)),
        ],
        out_specs=pl.BlockSpec((TM_OUT, D_MODEL), lambda i: (i, 0)),
        out_shape=jax.ShapeDtypeStruct((m, D_MODEL), F32),
        compiler_params=pltpu.CompilerParams(
            dimension_semantics=("arbitrary",), vmem_limit_bytes=VMEM_LIMIT),
        name="out_proj",
    )(ya, yb, yc, w_out, x2, g_post)


def _rope_tables(seq):
    pos = jnp.arange(seq, dtype=F32)
    inv_freq = ROPE_THETA ** (-jnp.arange(0, ROT_DIM, 2, dtype=F32) / ROT_DIM)
    ang = pos[:, None] * inv_freq[None, :]
    cos, sin = jnp.cos(ang), jnp.sin(ang)
    half = ROT_DIM // 2
    ones = jnp.ones((seq, HEAD_DIM - ROT_DIM), F32)
    zeros_rest = jnp.zeros((seq, HEAD_DIM - ROT_DIM), F32)
    zeros_half = jnp.zeros((seq, half), F32)
    rc = jnp.concatenate([cos, cos, ones], axis=-1)
    rs1 = jnp.concatenate([zeros_half, sin, zeros_rest], axis=-1)
    rs2 = jnp.concatenate([-sin, zeros_half, zeros_rest], axis=-1)
    tile = lambda t: jnp.concatenate([t, t], axis=-1)
    return tile(rc), tile(rs1), tile(rs2)


def _q_column_scale():
    scale = LOG2E * HEAD_DIM ** -0.5
    per_group = [1.0] * 12
    per_group[0], per_group[4], per_group[8] = scale, -scale, scale
    return jnp.concatenate([jnp.repeat(jnp.asarray(per_group, F32), W_GROUP), jnp.ones((N_HEADS_FOX,), F32)])


def _layer(x2, batch, seq, w_all, w_f, f_bias, lam_vec, subln, w_out, g_pre, g_post, layer_idx, tables):
    bias = jnp.pad(f_bias, (0, LANES - N_HEADS_FOX)).reshape(1, LANES)

    p, ff = _in_proj(x2, g_pre.reshape(1, D_MODEL), w_all, layer_idx, w_f, *tables, seq)
    p3 = p.reshape(batch, seq, D_MAIN)
    cum = _cumsum(ff.reshape(batch, seq, LANES), bias)[:, :, :N_HEADS_FOX]
    pairs = N_HEADS_FOX // 2
    cum_rep = jnp.repeat(cum.reshape(batch, seq, pairs, 2).transpose(0, 2, 1, 3), HEAD_DIM, axis=-1)

    lambda_init = 0.8 - 0.6 * math.exp(-0.3 * layer_idx)
    ya = _diff_attention(p3, lam_vec, subln.reshape(1, LANES), lambda_init)
    yb = _sb_attention(p3)
    yc = _fox_attention(p3, cum_rep)

    m = batch * seq
    return _out_proj(ya.reshape(m, W_GROUP), yb.reshape(m, W_GROUP), yc.reshape(m, W_GROUP),
                     w_out, x2, g_post.reshape(1, D_MODEL))


def kernel(x, w_in, forget_bias, diff_lambda, diff_subln, w_out, pre_norm, post_norm):
    batch, seq, d = x.shape
    tables = _rope_tables(seq)
    w_all = (w_in * _q_column_scale()).astype(BF16)
    w_f = jnp.pad(w_all[:, :, D_MAIN:], ((0, 0), (0, 0), (0, LANES - N_HEADS_FOX)))
    w_out = w_out.astype(BF16)
    x2 = x.reshape(batch * seq, d)
    for l in range(w_in.shape[0]):
        x2 = _layer(x2, batch, seq, w_all, w_f[l], forget_bias[l], diff_lambda[l], diff_subln[l],
                    w_out[l], pre_norm[l], post_norm[l], l, tables)
    return x2.reshape(batch, seq, d)
```

```python
import functools
import math

import jax
import jax.numpy as jnp
from jax import lax
from jax.experimental import pallas as pl
from jax.experimental.pallas import tpu as pltpu

F32 = jnp.float32
BF16 = jnp.bfloat16

D_MODEL = 1024
HEAD_DIM = 64
CHUNK = 64
ROT_DIM = HEAD_DIM // 4
ROPE_THETA = 500000.0
NORM_EPS = 1e-6
W_GROUP = 512
N_HEADS_FOX = 8
D_MAIN = 12 * W_GROUP
D_MIX = 3 * W_GROUP
LOG2E = math.log2(math.e)

LANES = 128
GROUP_BLOCKS = W_GROUP // LANES
VMEM_LIMIT = 56 * 1024 * 1024

TM_IN = 2048
TN_IN = 1024
TM_OUT = 1024
TQ = 1024
TK = 512
TK_SB = 256
CS_BLOCK = 256
ONES_ROWS = 16
SB_SEG = 256
SB_DEAD = 160.0
NEG_BIG = -1e30


def _dot(a, b):
    return jnp.dot(a, b, preferred_element_type=F32)


def _dot_nt(a, b):
    return lax.dot_general(a, b, (((1,), (1,)), ((), ())), preferred_element_type=F32)


def _softplus(z):
    return jnp.maximum(z, 0.0) + jnp.log(1.0 + jnp.exp(-jnp.abs(z)))


def _silu(g):
    return g / (1.0 + jnp.exp(-g))


def _in_proj_kernel(x_ref, g_ref, w_ref, wf_ref, rc_ref, rs1_ref, rs2_ref, p_ref, ff_ref, h_ref):
    j = pl.program_id(1)

    @pl.when(j == 0)
    def _():
        x = x_ref[...]
        ms = jnp.mean(x * x, axis=-1, keepdims=True)
        h = (x * lax.rsqrt(ms + NORM_EPS) * g_ref[...]).astype(BF16)
        h_ref[...] = h
        ff_ref[...] = _dot_nt(h, wf_ref[...])
        rc, rs1, rs2 = rc_ref[...], rs1_ref[...], rs2_ref[...]
        acc = _dot_nt(h, w_ref[...])
        for c in range(TN_IN // LANES):
            a = acc[:, c * LANES:(c + 1) * LANES]
            r = a * rc + pltpu.roll(a, ROT_DIM // 2, 1) * rs1 + pltpu.roll(a, LANES - ROT_DIM // 2, 1) * rs2
            p_ref[:, c * LANES:(c + 1) * LANES] = r.astype(BF16)

    @pl.when(j != 0)
    def _():
        p_ref[...] = _dot_nt(h_ref[...], w_ref[...]).astype(BF16)


def _in_proj(x2, g_pre, w_all, layer, w_f, rc, rs1, rs2, seq):
    m = x2.shape[0]
    n_pos_tiles = seq // TM_IN
    return pl.pallas_call(
        _in_proj_kernel,
        grid=(m // TM_IN, D_MAIN // TN_IN),
        in_specs=[
            pl.BlockSpec((TM_IN, D_MODEL), lambda i, j: (i, 0)),
            pl.BlockSpec((1, D_MODEL), lambda i, j: (0, 0)),
            pl.BlockSpec((None, TN_IN, D_MODEL), lambda i, j: (layer, j, 0)),
            pl.BlockSpec((LANES, D_MODEL), lambda i, j: (0, 0)),
            pl.BlockSpec((TM_IN, LANES), lambda i, j: (i % n_pos_tiles, 0)),
            pl.BlockSpec((TM_IN, LANES), lambda i, j: (i % n_pos_tiles, 0)),
            pl.BlockSpec((TM_IN, LANES), lambda i, j: (i % n_pos_tiles, 0)),
        ],
        out_specs=[
            pl.BlockSpec((TM_IN, TN_IN), lambda i, j: (i, j)),
            pl.BlockSpec((TM_IN, LANES), lambda i, j: (i, 0)),
        ],
        out_shape=[
            jax.ShapeDtypeStruct((m, D_MAIN), BF16),
            jax.ShapeDtypeStruct((m, LANES), F32),
        ],
        scratch_shapes=[pltpu.VMEM((TM_IN, D_MODEL), BF16)],
        compiler_params=pltpu.CompilerParams(
            dimension_semantics=("arbitrary", "arbitrary"), vmem_limit_bytes=VMEM_LIMIT),
        name="in_proj",
    )(x2, g_pre, w_all, w_f, rc, rs1, rs2)


def _cumsum_kernel(ff_ref, b_ref, o_ref):
    seq = ff_ref.shape[1]
    row = lax.broadcasted_iota(jnp.int32, (CS_BLOCK, CS_BLOCK), 0)
    col = lax.broadcasted_iota(jnp.int32, (CS_BLOCK, CS_BLOCK), 1)
    tri = (col <= row).astype(BF16)

    def body(c, carry):
        r0 = pl.multiple_of(c * CS_BLOCK, CS_BLOCK)
        z = ff_ref[0, pl.ds(r0, CS_BLOCK), :] + b_ref[...]
        lf = -_softplus(-z) * LOG2E
        hi = lf.astype(BF16)
        r1 = lf - hi.astype(F32)
        mid = r1.astype(BF16)
        lo = (r1 - mid.astype(F32)).astype(BF16)
        cs = _dot(tri, hi) + _dot(tri, mid) + _dot(tri, lo) + carry
        o_ref[0, pl.ds(r0, CS_BLOCK), :] = cs
        return cs[CS_BLOCK - 1:CS_BLOCK, :]

    lax.fori_loop(0, seq // CS_BLOCK, body, jnp.zeros((1, LANES), F32))


def _cumsum(ff3, bias):
    b, seq, _ = ff3.shape
    return pl.pallas_call(
        _cumsum_kernel,
        grid=(b,),
        in_specs=[
            pl.BlockSpec((1, seq, LANES), lambda i: (i, 0, 0)),
            pl.BlockSpec((1, LANES), lambda i: (0, 0)),
        ],
        out_specs=pl.BlockSpec((1, seq, LANES), lambda i: (i, 0, 0)),
        out_shape=jax.ShapeDtypeStruct((b, seq, LANES), F32),
        compiler_params=pltpu.CompilerParams(
            dimension_semantics=("arbitrary",), vmem_limit_bytes=VMEM_LIMIT),
        name="cum_log_forget",
    )(ff3, bias)


def _lane_lo():
    return lax.broadcasted_iota(jnp.int32, (1, LANES), 1) < HEAD_DIM


def _split_heads(q):
    lo = _lane_lo()
    zero = jnp.zeros_like(q)
    return jnp.where(lo, q, zero), jnp.where(lo, zero, q)


def _fill_v_transposed(v_ref, vt_ref, head_rows=LANES):
    tk = vt_ref.shape[2]

    @pl.when(pl.program_id(2) == 0)
    def _():
        for c in range(vt_ref.shape[0]):
            vt = v_ref[0, c * tk:(c + 1) * tk, :].astype(F32).T.astype(BF16)
            if vt_ref.shape[1] == LANES:
                vt_ref[c] = vt
                continue
            stride = head_rows + ONES_ROWS
            for h in range(LANES // head_rows):
                vt_ref[c, h * stride:h * stride + head_rows, :] = vt[h * head_rows:(h + 1) * head_rows]
                vt_ref[c, h * stride + head_rows:(h + 1) * stride, :] = jnp.ones((ONES_ROWS, tk), BF16)


def _softmax_step(state, s, vt):
    m, acc = state
    m_new = jnp.maximum(m, jnp.max(s, axis=0, keepdims=True))
    alpha = jnp.exp2(m - m_new)
    p = jnp.exp2(s - m_new)
    acc = alpha * acc + _dot(vt, p.astype(BF16))
    return m_new, acc


def _softmax_result(state, d):
    acc = state[1]
    return acc[:d] / acc[d:d + 1]


def _diag_update(state, d, kj, scores, update, visible, tk=TK):
    first = d * tk
    key = lax.broadcasted_iota(jnp.int32, (tk, TQ - first), 0)
    qry = lax.broadcasted_iota(jnp.int32, (tk, TQ - first), 1)
    lanes = slice(first, TQ)
    touched = jax.tree.map(lambda a: a[:, lanes], state)
    touched = update(touched, scores(kj, lanes), kj, visible(key, qry))
    if first == 0:
        return touched
    return jax.tree.map(lambda a, b: jnp.concatenate([a[:, :first], b], axis=1), state, touched)


def _sweep_diagonal_last(i, scores, update, state, visible):
    per_trip = TQ // TK
    n_off = per_trip * i

    def body(t, st):
        for d in range(per_trip):
            st = update(st, scores(per_trip * t + d, slice(None)), per_trip * t + d, None)
        return st

    state = lax.fori_loop(0, i, body, state)
    for d in range(per_trip):
        state = _diag_update(state, d, n_off + d, scores, update, visible)
    return state


def _softmax_init(d):
    return (jnp.full((1, TQ), NEG_BIG, F32), jnp.zeros((d + ONES_ROWS, TQ), F32))


def _attn_specs(q_blk, k_blk, v_blk, g_blk, seq):
    return [
        pl.BlockSpec((1, TQ, LANES), lambda b, u, i: (b, i, q_blk + u)),
        pl.BlockSpec((1, seq, LANES), lambda b, u, i: (b, 0, k_blk + u)),
        pl.BlockSpec((1, seq, LANES), lambda b, u, i: (b, 0, v_blk + u)),
        pl.BlockSpec((1, TQ, LANES), lambda b, u, i: (b, i, g_blk + u)),
    ]


def _vt_scratch(seq, tk=TK, head_rows=None):
    rows = LANES if head_rows is None else (LANES // head_rows) * (head_rows + ONES_ROWS)
    return pltpu.VMEM((seq // tk, rows, tk), BF16)


_ATTN_PARAMS = pltpu.CompilerParams(
    dimension_semantics=("arbitrary", "arbitrary", "arbitrary"), vmem_limit_bytes=VMEM_LIMIT)


def _diff_kernel(q_ref, k_ref, v_ref, g_ref, lam_ref, sub_ref, o_ref, vt_ref, *, lambda_init):
    i = pl.program_id(2)
    _fill_v_transposed(v_ref, vt_ref, LANES)
    q0, q1 = _split_heads(q_ref[0])

    def scores(kj, queries):
        k0 = pl.multiple_of(kj * TK, TK)
        k = k_ref[0, pl.ds(k0, TK), :]
        return _dot_nt(k, q0[queries]), _dot_nt(k, q1[queries])

    def update(state, s, kj, mask):
        vt = vt_ref[kj]
        s0, s1 = s
        if mask is not None:
            s0, s1 = jnp.where(mask, s0, NEG_BIG), jnp.where(mask, s1, NEG_BIG)
        return _softmax_step(state[0], s0, vt), _softmax_step(state[1], s1, vt)

    visible = lambda key, qry: (key // CHUNK) <= (qry // CHUNK)
    init = (_softmax_init(LANES), _softmax_init(LANES))
    st0, st1 = _sweep_diagonal_last(i, scores, update, init, visible)

    lv = lam_ref[...]
    lam = (jnp.exp(jnp.sum(lv[0:1] * lv[1:2], axis=-1, keepdims=True))
           - jnp.exp(jnp.sum(lv[2:3] * lv[3:4], axis=-1, keepdims=True)) + lambda_init)
    o = (_softmax_result(st0, LANES) - lam * _softmax_result(st1, LANES)).T
    o = o * lax.rsqrt(jnp.mean(o * o, axis=-1, keepdims=True) + NORM_EPS) * sub_ref[...]
    o = o * (1.0 - lambda_init)
    o_ref[0] = (o * _silu(g_ref[0].astype(F32))).astype(BF16)


def _diff_attention(p3, lam_vec, subln, lambda_init):
    b, seq, _ = p3.shape
    gb = GROUP_BLOCKS
    return pl.pallas_call(
        functools.partial(_diff_kernel, lambda_init=lambda_init),
        grid=(b, gb, seq // TQ),
        in_specs=_attn_specs(0 * gb, 1 * gb, 2 * gb, 3 * gb, seq) + [
            pl.BlockSpec((4, HEAD_DIM), lambda b, u, i: (0, 0)),
            pl.BlockSpec((1, LANES), lambda b, u, i: (0, 0)),
        ],
        out_specs=pl.BlockSpec((1, TQ, LANES), lambda b, u, i: (b, i, u)),
        out_shape=jax.ShapeDtypeStruct((b, seq, W_GROUP), BF16),
        scratch_shapes=[_vt_scratch(seq, TK, LANES)],
        compiler_params=_ATTN_PARAMS,
        name="diff_attention",
    )(p3, p3, p3, p3, lam_vec, subln)


def _sb_kernel(q_ref, k_ref, v_ref, g_ref, o_ref, vt_ref):
    i = pl.program_id(2)
    _fill_v_transposed(v_ref, vt_ref)
    q0, q1 = _split_heads(q_ref[0])
    row = lax.broadcasted_iota(jnp.int32, (SB_SEG, SB_SEG), 0)
    col = lax.broadcasted_iota(jnp.int32, (SB_SEG, SB_SEG), 1)
    after = (col > row).astype(BF16)
    def head(y, vt, state, mask):
        rest_prev, acc = state
        neg_part = jnp.minimum(y, 0.0)
        neg_abs = neg_part + (neg_part - y)
        log_keep = neg_part - jnp.log2(1.0 + jnp.exp2(neg_abs))
        if mask is not None:
            log_keep = jnp.where(mask, log_keep, 0.0)
        lkb = log_keep.astype(BF16)
        rests, later = [], rest_prev
        for g in reversed(range(TK_SB // SB_SEG)):
            rows = slice(g * SB_SEG, (g + 1) * SB_SEG)
            suffix = _dot(after, lkb[rows])
            rests.insert(0, suffix + later)
            later = later + (suffix[0:1] + log_keep[g * SB_SEG:g * SB_SEG + 1])
        rest = rests[0] if len(rests) == 1 else jnp.concatenate(rests, axis=0)
        a = jnp.exp2(log_keep - y + rest)
        if mask is not None:
            a = jnp.where(mask, a, 0.0)
        return later, acc + _dot(vt, a.astype(BF16))

    def scores(kj, queries):
        k0 = pl.multiple_of(kj * TK_SB, TK_SB)
        k = k_ref[0, pl.ds(k0, TK_SB), :]
        return _dot_nt(k, q0[queries]), _dot_nt(k, q1[queries])

    def update(state, y, kj, mask):
        vt = vt_ref[kj]
        return head(y[0], vt[:HEAD_DIM], state[0], mask), head(y[1], vt[HEAD_DIM:], state[1], mask)

    init = (jnp.zeros((1, TQ), F32), jnp.zeros((HEAD_DIM, TQ), F32))
    state = (init, init)
    n_diag = TQ // TK_SB
    n_off = n_diag * i
    for d in reversed(range(n_diag)):
        state = _diag_update(state, d, n_off + d, scores, update, lambda key, qry: key < qry, TK_SB)

    def cond(carry):
        t, st = carry
        reach = jnp.maximum(jnp.max(st[0][0]), jnp.max(st[1][0]))
        return jnp.logical_and(t < n_off, reach > -SB_DEAD)

    def body(carry):
        t, st = carry
        kj = n_off - 1 - t
        return t + 1, update(st, scores(kj, slice(None)), kj, None)

    _, (st0, st1) = lax.while_loop(cond, body, (jnp.int32(0), state))
    o = jnp.concatenate([st0[1], st1[1]], axis=0).T
    o_ref[0] = (o * _silu(g_ref[0].astype(F32))).astype(BF16)


def _sb_attention(p3):
    b, seq, _ = p3.shape
    gb = GROUP_BLOCKS
    return pl.pallas_call(
        _sb_kernel,
        grid=(b, gb, seq // TQ),
        in_specs=_attn_specs(4 * gb, 5 * gb, 6 * gb, 7 * gb, seq),
        out_specs=pl.BlockSpec((1, TQ, LANES), lambda b, u, i: (b, i, u)),
        out_shape=jax.ShapeDtypeStruct((b, seq, W_GROUP), BF16),
        scratch_shapes=[_vt_scratch(seq, TK_SB)],
        compiler_params=_ATTN_PARAMS,
        name="sb_attention",
    )(p3, p3, p3, p3)


def _fox_kernel(q_ref, k_ref, v_ref, g_ref, c_ref, o_ref, vt_ref, kaug_ref):
    i = pl.program_id(2)
    seq = k_ref.shape[1]
    _fill_v_transposed(v_ref, vt_ref, HEAD_DIM)
    lane = lax.broadcasted_iota(jnp.int32, (1, LANES), 1)
    slot = lane % HEAD_DIM

    def pieces(c):
        hi = c.astype(BF16).astype(F32)
        rem = c - hi
        mid = rem.astype(BF16).astype(F32)
        return hi, mid, (rem - mid).astype(BF16).astype(F32)

    @pl.when(i == 0)
    def _():
        for blk in range(seq // TK):
            hi, mid, lo = pieces(c_ref[0, blk * TK:(blk + 1) * TK, :])
            kaug = jnp.where(slot < 3, 1.0, jnp.where(slot == 3, -hi, jnp.where(slot == 4, -mid, jnp.where(slot == 5, -lo, 0.0))))
            kaug_ref[blk * TK:(blk + 1) * TK, :] = kaug.astype(BF16)

    hi, mid, lo = pieces(c_ref[0, pl.ds(pl.multiple_of(i * TQ, TQ), TQ), :])
    qaug = jnp.where(slot == 0, hi, jnp.where(slot == 1, mid, jnp.where(slot == 2, lo, jnp.where(slot < 6, 1.0, 0.0))))
    q0, q1 = _split_heads(q_ref[0])
    qa0, qa1 = _split_heads(qaug.astype(BF16))
    q0, q1 = jnp.concatenate([q0, qa0], axis=1), jnp.concatenate([q1, qa1], axis=1)

    def scores(kj, queries):
        k0 = pl.multiple_of(kj * TK, TK)
        k = jnp.concatenate([k_ref[0, pl.ds(k0, TK), :], kaug_ref[pl.ds(k0, TK), :]], axis=1)
        return _dot_nt(k, q0[queries]), _dot_nt(k, q1[queries])

    def update(state, s, kj, mask):
        vt = vt_ref[kj]
        s0, s1 = s
        if mask is not None:
            s0, s1 = jnp.where(mask, s0, NEG_BIG), jnp.where(mask, s1, NEG_BIG)
        rows = HEAD_DIM + ONES_ROWS
        return _softmax_step(state[0], s0, vt[:rows]), _softmax_step(state[1], s1, vt[rows:])

    init = (_softmax_init(HEAD_DIM), _softmax_init(HEAD_DIM))
    st0, st1 = _sweep_diagonal_last(i, scores, update, init, lambda key, qry: key <= qry)
    o = jnp.concatenate([_softmax_result(st0, HEAD_DIM), _softmax_result(st1, HEAD_DIM)], axis=0).T
    o_ref[0] = (o * _silu(g_ref[0].astype(F32))).astype(BF16)


def _fox_attention(p3, cum_rep):
    b, seq, _ = p3.shape
    gb = GROUP_BLOCKS
    return pl.pallas_call(
        _fox_kernel,
        grid=(b, gb, seq // TQ),
        in_specs=_attn_specs(8 * gb, 9 * gb, 10 * gb, 11 * gb, seq) + [
            pl.BlockSpec((1, seq, LANES), lambda b, u, i: (b, 0, u)),
        ],
        out_specs=pl.BlockSpec((1, TQ, LANES), lambda b, u, i: (b, i, u)),
        out_shape=jax.ShapeDtypeStruct((b, seq, W_GROUP), BF16),
        scratch_shapes=[_vt_scratch(seq, TK, HEAD_DIM), pltpu.VMEM((seq, LANES), BF16)],
        compiler_params=_ATTN_PARAMS,
        name="fox_attention",
    )(p3, p3, p3, p3, cum_rep)


def _out_proj_kernel(ya_ref, yb_ref, yc_ref, w_ref, x_ref, g_ref, o_ref):
    y = (_dot(ya_ref[...], w_ref[0:W_GROUP, :])
         + _dot(yb_ref[...], w_ref[W_GROUP:2 * W_GROUP, :])
         + _dot(yc_ref[...], w_ref[2 * W_GROUP:3 * W_GROUP, :]))
    ms = jnp.mean(y * y, axis=-1, keepdims=True)
    o_ref[...] = x_ref[...] + y * lax.rsqrt(ms + NORM_EPS) * g_ref[...]


def _out_proj(ya, yb, yc, w_out, x2, g_post):
    m = x2.shape[0]
    y_spec = pl.BlockSpec((TM_OUT, W_GROUP), lambda i: (i, 0))
    return pl.pallas_call(
        _out_proj_kernel,
        grid=(m // TM_OUT,),
        in_specs=[
            y_spec, y_spec, y_spec,
            pl.BlockSpec((D_MIX, D_MODEL), lambda i: (0, 0)),
            pl.BlockSpec((TM_OUT, D_MODEL), lambda i: (i, 0)),
            pl.BlockSpec((1, D_MODEL), lambda i: (0, 0)),
        ],
        out_specs=pl.BlockSpec((TM_OUT, D_MODEL), lambda i: (i, 0)),
        out_shape=jax.ShapeDtypeStruct((m, D_MODEL), F32),
        compiler_params=pltpu.CompilerParams(
            dimension_semantics=("arbitrary",), vmem_limit_bytes=VMEM_LIMIT),
        name="out_proj",
    )(ya, yb, yc, w_out, x2, g_post)


def _rope_tables(seq):
    pos = jnp.arange(seq, dtype=F32)
    inv_freq = ROPE_THETA ** (-jnp.arange(0, ROT_DIM, 2, dtype=F32) / ROT_DIM)
    ang = pos[:, None] * inv_freq[None, :]
    cos, sin = jnp.cos(ang), jnp.sin(ang)
    half = ROT_DIM // 2
    ones = jnp.ones((seq, HEAD_DIM - ROT_DIM), F32)
    zeros_rest = jnp.zeros((seq, HEAD_DIM - ROT_DIM), F32)
    zeros_half = jnp.zeros((seq, half), F32)
    rc = jnp.concatenate([cos, cos, ones], axis=-1)
    rs1 = jnp.concatenate([zeros_half, sin, zeros_rest], axis=-1)
    rs2 = jnp.concatenate([-sin, zeros_half, zeros_rest], axis=-1)
    tile = lambda t: jnp.concatenate([t, t], axis=-1)
    return tile(rc), tile(rs1), tile(rs2)


def _q_column_scale():
    scale = LOG2E * HEAD_DIM ** -0.5
    per_group = [1.0] * 12
    per_group[0], per_group[4], per_group[8] = scale, -scale, scale
    return jnp.repeat(jnp.asarray(per_group, F32), W_GROUP)


def _cast_weights_kernel(w_ref, scale_ref, o_ref):
    o_ref[...] = (w_ref[...] * scale_ref[...]).astype(BF16)


def _cast_weights(w_t, row_scale):
    depth = w_t.shape[0]
    w_spec = pl.BlockSpec((None, TN_IN, D_MODEL), lambda l, j: (l, j, 0))
    return pl.pallas_call(
        _cast_weights_kernel,
        grid=(depth, D_MAIN // TN_IN),
        in_specs=[w_spec, pl.BlockSpec((TN_IN, 1), lambda l, j: (j, 0))],
        out_specs=w_spec,
        out_shape=jax.ShapeDtypeStruct((depth, D_MAIN, D_MODEL), BF16),
        compiler_params=pltpu.CompilerParams(
            dimension_semantics=("arbitrary", "arbitrary"), vmem_limit_bytes=VMEM_LIMIT),
        name="cast_weights",
    )(w_t, row_scale)


def _layer(x2, batch, seq, w_all, w_f, f_bias, lam_vec, subln, w_out, g_pre, g_post, layer_idx, tables):
    bias = jnp.pad(f_bias, (0, LANES - N_HEADS_FOX)).reshape(1, LANES)

    p, ff = _in_proj(x2, g_pre.reshape(1, D_MODEL), w_all, layer_idx, w_f, *tables, seq)
    p3 = p.reshape(batch, seq, D_MAIN)
    cum = _cumsum(ff.reshape(batch, seq, LANES), bias)[:, :, :N_HEADS_FOX]
    select = jnp.repeat(jnp.eye(N_HEADS_FOX, dtype=F32), HEAD_DIM, axis=1)
    cum_rep = jnp.einsum("bsh,hl->bsl", cum, select, precision=lax.Precision.HIGHEST)

    lambda_init = 0.8 - 0.6 * math.exp(-0.3 * layer_idx)
    ya = _diff_attention(p3, lam_vec, subln.reshape(1, LANES), lambda_init)
    yb = _sb_attention(p3)
    yc = _fox_attention(p3, cum_rep)

    m = batch * seq
    return _out_proj(ya.reshape(m, W_GROUP), yb.reshape(m, W_GROUP), yc.reshape(m, W_GROUP),
                     w_out, x2, g_post.reshape(1, D_MODEL))


def kernel(x, w_in, forget_bias, diff_lambda, diff_subln, w_out, pre_norm, post_norm):
    batch, seq, d = x.shape
    tables = _rope_tables(seq)
    w_t = jnp.transpose(w_in, (0, 2, 1))
    w_all = _cast_weights(w_t, _q_column_scale().reshape(D_MAIN, 1))
    w_f = jnp.pad(w_t[:, D_MAIN:, :], ((0, 0), (0, LANES - N_HEADS_FOX), (0, 0))).astype(BF16)
    w_out = w_out.astype(BF16)
    x2 = x.reshape(batch * seq, d)
    for l in range(w_in.shape[0]):
        x2 = _layer(x2, batch, seq, w_all, w_f[l], forget_bias[l], diff_lambda[l], diff_subln[l],
                    w_out[l], pre_norm[l], post_norm[l], l, tables)
    return x2.reshape(batch, seq, d)
```

```python
import functools
import math

import jax
import jax.numpy as jnp
from jax import lax
from jax.experimental import pallas as pl
from jax.experimental.pallas import tpu as pltpu

F32 = jnp.float32
BF16 = jnp.bfloat16

D_MODEL = 1024
HEAD_DIM = 64
CHUNK = 64
ROT_DIM = HEAD_DIM // 4
ROPE_THETA = 500000.0
NORM_EPS = 1e-6
W_GROUP = 512
N_HEADS_FOX = 8
D_MAIN = 12 * W_GROUP
D_MIX = 3 * W_GROUP
LOG2E = math.log2(math.e)

LANES = 128
GROUP_BLOCKS = W_GROUP // LANES
VMEM_LIMIT = 56 * 1024 * 1024

TM_IN = 2048
TN_IN = 1024
TM_OUT = 1024
TQ = 1024
TK = 512
TK_SB = 256
CS_BLOCK = 256
ONES_ROWS = 16
SB_SEG = 256
SB_DEAD = 160.0
NEG_BIG = -1e30


def _dot(a, b):
    return jnp.dot(a, b, preferred_element_type=F32)


def _dot_nt(a, b):
    return lax.dot_general(a, b, (((1,), (1,)), ((), ())), preferred_element_type=F32)


def _softplus(z):
    return jnp.maximum(z, 0.0) + jnp.log(1.0 + jnp.exp(-jnp.abs(z)))


def _silu(g):
    return g / (1.0 + jnp.exp(-g))


def _in_proj_kernel(x_ref, g_ref, w_ref, wf_ref, rc_ref, rs1_ref, rs2_ref, p_ref, ff_ref, h_ref):
    j = pl.program_id(1)

    @pl.when(j == 0)
    def _():
        x = x_ref[...]
        ms = jnp.mean(x * x, axis=-1, keepdims=True)
        h = (x * lax.rsqrt(ms + NORM_EPS) * g_ref[...]).astype(BF16)
        h_ref[...] = h
        ff_ref[...] = _dot_nt(h, wf_ref[...])
        rc, rs1, rs2 = rc_ref[...], rs1_ref[...], rs2_ref[...]
        acc = _dot_nt(h, w_ref[...])
        for c in range(TN_IN // LANES):
            a = acc[:, c * LANES:(c + 1) * LANES]
            r = a * rc + pltpu.roll(a, ROT_DIM // 2, 1) * rs1 + pltpu.roll(a, LANES - ROT_DIM // 2, 1) * rs2
            p_ref[:, c * LANES:(c + 1) * LANES] = r.astype(BF16)

    @pl.when(j != 0)
    def _():
        p_ref[...] = _dot_nt(h_ref[...], w_ref[...]).astype(BF16)


def _in_proj(x2, g_pre, w_all, layer, w_f, rc, rs1, rs2, seq):
    m = x2.shape[0]
    n_pos_tiles = seq // TM_IN
    return pl.pallas_call(
        _in_proj_kernel,
        grid=(m // TM_IN, D_MAIN // TN_IN),
        in_specs=[
            pl.BlockSpec((TM_IN, D_MODEL), lambda i, j: (i, 0)),
            pl.BlockSpec((1, D_MODEL), lambda i, j: (0, 0)),
            pl.BlockSpec((None, TN_IN, D_MODEL), lambda i, j: (layer, j, 0)),
            pl.BlockSpec((LANES, D_MODEL), lambda i, j: (0, 0)),
            pl.BlockSpec((TM_IN, LANES), lambda i, j: (i % n_pos_tiles, 0)),
            pl.BlockSpec((TM_IN, LANES), lambda i, j: (i % n_pos_tiles, 0)),
            pl.BlockSpec((TM_IN, LANES), lambda i, j: (i % n_pos_tiles, 0)),
        ],
        out_specs=[
            pl.BlockSpec((TM_IN, TN_IN), lambda i, j: (i, j)),
            pl.BlockSpec((TM_IN, LANES), lambda i, j: (i, 0)),
        ],
        out_shape=[
            jax.ShapeDtypeStruct((m, D_MAIN), BF16),
            jax.ShapeDtypeStruct((m, LANES), F32),
        ],
        scratch_shapes=[pltpu.VMEM((TM_IN, D_MODEL), BF16)],
        compiler_params=pltpu.CompilerParams(
            dimension_semantics=("arbitrary", "arbitrary"), vmem_limit_bytes=VMEM_LIMIT),
        name="in_proj",
    )(x2, g_pre, w_all, w_f, rc, rs1, rs2)


def _bf16_pieces(a):
    hi = a.astype(BF16)
    rem = a - hi.astype(F32)
    mid = rem.astype(BF16)
    return hi, mid, (rem - mid.astype(F32)).astype(BF16)


def _cumsum_kernel(ff_ref, b_ref, o_ref):
    seq = ff_ref.shape[1]
    row = lax.broadcasted_iota(jnp.int32, (CS_BLOCK, CS_BLOCK), 0)
    col = lax.broadcasted_iota(jnp.int32, (CS_BLOCK, CS_BLOCK), 1)
    tri = (col <= row).astype(BF16)
    head = lax.broadcasted_iota(jnp.int32, (LANES, N_HEADS_FOX * HEAD_DIM), 0)
    lane = lax.broadcasted_iota(jnp.int32, (LANES, N_HEADS_FOX * HEAD_DIM), 1)
    spread = (lane // HEAD_DIM == head).astype(BF16)

    def body(c, carry):
        r0 = pl.multiple_of(c * CS_BLOCK, CS_BLOCK)
        z = ff_ref[0, pl.ds(r0, CS_BLOCK), :] + b_ref[...]
        lf = -_softplus(-z) * LOG2E
        cs = sum(_dot(tri, piece) for piece in _bf16_pieces(lf)) + carry
        o_ref[0, pl.ds(r0, CS_BLOCK), :] = sum(_dot(piece, spread) for piece in _bf16_pieces(cs))
        return cs[CS_BLOCK - 1:CS_BLOCK, :]

    lax.fori_loop(0, seq // CS_BLOCK, body, jnp.zeros((1, LANES), F32))


def _cumsum(ff3, bias):
    b, seq, _ = ff3.shape
    return pl.pallas_call(
        _cumsum_kernel,
        grid=(b,),
        in_specs=[
            pl.BlockSpec((1, seq, LANES), lambda i: (i, 0, 0)),
            pl.BlockSpec((1, LANES), lambda i: (0, 0)),
        ],
        out_specs=pl.BlockSpec((1, seq, N_HEADS_FOX * HEAD_DIM), lambda i: (i, 0, 0)),
        out_shape=jax.ShapeDtypeStruct((b, seq, N_HEADS_FOX * HEAD_DIM), F32),
        compiler_params=pltpu.CompilerParams(
            dimension_semantics=("arbitrary",), vmem_limit_bytes=VMEM_LIMIT),
        name="cum_log_forget",
    )(ff3, bias)


def _lane_lo():
    return lax.broadcasted_iota(jnp.int32, (1, LANES), 1) < HEAD_DIM


def _split_heads(q):
    lo = _lane_lo()
    zero = jnp.zeros_like(q)
    return jnp.where(lo, q, zero), jnp.where(lo, zero, q)


def _fill_v_transposed(v_ref, vt_ref, head_rows=LANES):
    tk = vt_ref.shape[2]

    @pl.when(pl.program_id(2) == 0)
    def _():
        for c in range(vt_ref.shape[0]):
            vt = v_ref[0, c * tk:(c + 1) * tk, :].astype(F32).T.astype(BF16)
            if vt_ref.shape[1] == LANES:
                vt_ref[c] = vt
                continue
            stride = head_rows + ONES_ROWS
            for h in range(LANES // head_rows):
                vt_ref[c, h * stride:h * stride + head_rows, :] = vt[h * head_rows:(h + 1) * head_rows]
                vt_ref[c, h * stride + head_rows:(h + 1) * stride, :] = jnp.ones((ONES_ROWS, tk), BF16)


def _softmax_step(state, s, vt):
    m, acc = state
    m_new = jnp.maximum(m, jnp.max(s, axis=0, keepdims=True))
    alpha = jnp.exp2(m - m_new)
    p = jnp.exp2(s - m_new)
    acc = alpha * acc + _dot(vt, p.astype(BF16))
    return m_new, acc


def _softmax_result(state, d):
    acc = state[1]
    return acc[:d] / acc[d:d + 1]


def _diag_update(state, d, kj, scores, update, visible, tk=TK):
    first = d * tk
    key = lax.broadcasted_iota(jnp.int32, (tk, TQ - first), 0)
    qry = lax.broadcasted_iota(jnp.int32, (tk, TQ - first), 1)
    lanes = slice(first, TQ)
    touched = jax.tree.map(lambda a: a[:, lanes], state)
    touched = update(touched, scores(kj, lanes), kj, visible(key, qry))
    if first == 0:
        return touched
    return jax.tree.map(lambda a, b: jnp.concatenate([a[:, :first], b], axis=1), state, touched)


def _sweep_diagonal_last(i, scores, update, state, visible):
    per_trip = TQ // TK
    n_off = per_trip * i

    def body(t, st):
        for d in range(per_trip):
            st = update(st, scores(per_trip * t + d, slice(None)), per_trip * t + d, None)
        return st

    state = lax.fori_loop(0, i, body, state)
    for d in range(per_trip):
        state = _diag_update(state, d, n_off + d, scores, update, visible)
    return state


def _softmax_init(d):
    return (jnp.full((1, TQ), NEG_BIG, F32), jnp.zeros((d + ONES_ROWS, TQ), F32))


def _attn_specs(q_blk, k_blk, v_blk, g_blk, seq):
    return [
        pl.BlockSpec((1, TQ, LANES), lambda b, u, i: (b, i, q_blk + u)),
        pl.BlockSpec((1, seq, LANES), lambda b, u, i: (b, 0, k_blk + u)),
        pl.BlockSpec((1, seq, LANES), lambda b, u, i: (b, 0, v_blk + u)),
        pl.BlockSpec((1, TQ, LANES), lambda b, u, i: (b, i, g_blk + u)),
    ]


def _vt_scratch(seq, tk=TK, head_rows=None):
    rows = LANES if head_rows is None else (LANES // head_rows) * (head_rows + ONES_ROWS)
    return pltpu.VMEM((seq // tk, rows, tk), BF16)


_ATTN_PARAMS = pltpu.CompilerParams(
    dimension_semantics=("arbitrary", "arbitrary", "arbitrary"), vmem_limit_bytes=VMEM_LIMIT)


def _diff_kernel(q_ref, k_ref, v_ref, g_ref, lam_ref, sub_ref, o_ref, vt_ref, *, lambda_init):
    i = pl.program_id(2)
    _fill_v_transposed(v_ref, vt_ref, LANES)
    q0, q1 = _split_heads(q_ref[0])

    def scores(kj, queries):
        k0 = pl.multiple_of(kj * TK, TK)
        k = k_ref[0, pl.ds(k0, TK), :]
        return _dot_nt(k, q0[queries]), _dot_nt(k, q1[queries])

    def update(state, s, kj, mask):
        vt = vt_ref[kj]
        s0, s1 = s
        if mask is not None:
            s0, s1 = jnp.where(mask, s0, NEG_BIG), jnp.where(mask, s1, NEG_BIG)
        return _softmax_step(state[0], s0, vt), _softmax_step(state[1], s1, vt)

    visible = lambda key, qry: (key // CHUNK) <= (qry // CHUNK)
    init = (_softmax_init(LANES), _softmax_init(LANES))
    st0, st1 = _sweep_diagonal_last(i, scores, update, init, visible)

    lv = lam_ref[...]
    lam = (jnp.exp(jnp.sum(lv[0:1] * lv[1:2], axis=-1, keepdims=True))
           - jnp.exp(jnp.sum(lv[2:3] * lv[3:4], axis=-1, keepdims=True)) + lambda_init)
    o = (_softmax_result(st0, LANES) - lam * _softmax_result(st1, LANES)).T
    o = o * lax.rsqrt(jnp.mean(o * o, axis=-1, keepdims=True) + NORM_EPS) * sub_ref[...]
    o = o * (1.0 - lambda_init)
    o_ref[0] = (o * _silu(g_ref[0].astype(F32))).astype(BF16)


def _diff_attention(p3, lam_vec, subln, lambda_init):
    b, seq, _ = p3.shape
    gb = GROUP_BLOCKS
    return pl.pallas_call(
        functools.partial(_diff_kernel, lambda_init=lambda_init),
        grid=(b, gb, seq // TQ),
        in_specs=_attn_specs(0 * gb, 1 * gb, 2 * gb, 3 * gb, seq) + [
            pl.BlockSpec((4, HEAD_DIM), lambda b, u, i: (0, 0)),
            pl.BlockSpec((1, LANES), lambda b, u, i: (0, 0)),
        ],
        out_specs=pl.BlockSpec((1, TQ, LANES), lambda b, u, i: (b, i, u)),
        out_shape=jax.ShapeDtypeStruct((b, seq, W_GROUP), BF16),
        scratch_shapes=[_vt_scratch(seq, TK, LANES)],
        compiler_params=_ATTN_PARAMS,
        name="diff_attention",
    )(p3, p3, p3, p3, lam_vec, subln)


def _sb_kernel(q_ref, k_ref, v_ref, g_ref, o_ref, vt_ref):
    i = pl.program_id(2)
    _fill_v_transposed(v_ref, vt_ref)
    q0, q1 = _split_heads(q_ref[0])
    row = lax.broadcasted_iota(jnp.int32, (SB_SEG, SB_SEG), 0)
    col = lax.broadcasted_iota(jnp.int32, (SB_SEG, SB_SEG), 1)
    after = (col > row).astype(BF16)
    def head(y, vt, state, mask):
        rest_prev, acc = state
        neg_part = jnp.minimum(y, 0.0)
        neg_abs = neg_part + (neg_part - y)
        log_keep = neg_part - jnp.log2(1.0 + jnp.exp2(neg_abs))
        if mask is not None:
            log_keep = jnp.where(mask, log_keep, 0.0)
        lkb = log_keep.astype(BF16)
        rests, later = [], rest_prev
        for g in reversed(range(TK_SB // SB_SEG)):
            rows = slice(g * SB_SEG, (g + 1) * SB_SEG)
            suffix = _dot(after, lkb[rows])
            rests.insert(0, suffix + later)
            later = later + (suffix[0:1] + log_keep[g * SB_SEG:g * SB_SEG + 1])
        rest = rests[0] if len(rests) == 1 else jnp.concatenate(rests, axis=0)
        a = jnp.exp2(log_keep - y + rest)
        if mask is not None:
            a = jnp.where(mask, a, 0.0)
        return later, acc + _dot(vt, a.astype(BF16))

    def scores(kj, queries):
        k0 = pl.multiple_of(kj * TK_SB, TK_SB)
        k = k_ref[0, pl.ds(k0, TK_SB), :]
        return _dot_nt(k, q0[queries]), _dot_nt(k, q1[queries])

    def update(state, y, kj, mask):
        vt = vt_ref[kj]
        return head(y[0], vt[:HEAD_DIM], state[0], mask), head(y[1], vt[HEAD_DIM:], state[1], mask)

    init = (jnp.zeros((1, TQ), F32), jnp.zeros((HEAD_DIM, TQ), F32))
    state = (init, init)
    n_diag = TQ // TK_SB
    n_off = n_diag * i
    for d in reversed(range(n_diag)):
        state = _diag_update(state, d, n_off + d, scores, update, lambda key, qry: key < qry, TK_SB)

    def cond(carry):
        t, st = carry
        reach = jnp.maximum(jnp.max(st[0][0]), jnp.max(st[1][0]))
        return jnp.logical_and(t < n_off, reach > -SB_DEAD)

    def body(carry):
        t, st = carry
        kj = n_off - 1 - t
        return t + 1, update(st, scores(kj, slice(None)), kj, None)

    _, (st0, st1) = lax.while_loop(cond, body, (jnp.int32(0), state))
    o = jnp.concatenate([st0[1], st1[1]], axis=0).T
    o_ref[0] = (o * _silu(g_ref[0].astype(F32))).astype(BF16)


def _sb_attention(p3):
    b, seq, _ = p3.shape
    gb = GROUP_BLOCKS
    return pl.pallas_call(
        _sb_kernel,
        grid=(b, gb, seq // TQ),
        in_specs=_attn_specs(4 * gb, 5 * gb, 6 * gb, 7 * gb, seq),
        out_specs=pl.BlockSpec((1, TQ, LANES), lambda b, u, i: (b, i, u)),
        out_shape=jax.ShapeDtypeStruct((b, seq, W_GROUP), BF16),
        scratch_shapes=[_vt_scratch(seq, TK_SB)],
        compiler_params=_ATTN_PARAMS,
        name="sb_attention",
    )(p3, p3, p3, p3)


def _fox_kernel(q_ref, k_ref, v_ref, g_ref, c_ref, o_ref, vt_ref, kaug_ref):
    i = pl.program_id(2)
    seq = k_ref.shape[1]
    _fill_v_transposed(v_ref, vt_ref, HEAD_DIM)
    lane = lax.broadcasted_iota(jnp.int32, (1, LANES), 1)
    slot = lane % HEAD_DIM

    def pieces(c):
        hi = c.astype(BF16).astype(F32)
        rem = c - hi
        mid = rem.astype(BF16).astype(F32)
        return hi, mid, (rem - mid).astype(BF16).astype(F32)

    @pl.when(i == 0)
    def _():
        for blk in range(seq // TK):
            hi, mid, lo = pieces(c_ref[0, blk * TK:(blk + 1) * TK, :])
            kaug = jnp.where(slot < 3, 1.0, jnp.where(slot == 3, -hi, jnp.where(slot == 4, -mid, jnp.where(slot == 5, -lo, 0.0))))
            kaug_ref[blk * TK:(blk + 1) * TK, :] = kaug.astype(BF16)

    hi, mid, lo = pieces(c_ref[0, pl.ds(pl.multiple_of(i * TQ, TQ), TQ), :])
    qaug = jnp.where(slot == 0, hi, jnp.where(slot == 1, mid, jnp.where(slot == 2, lo, jnp.where(slot < 6, 1.0, 0.0))))
    q0, q1 = _split_heads(q_ref[0])
    qa0, qa1 = _split_heads(qaug.astype(BF16))
    q0, q1 = jnp.concatenate([q0, qa0], axis=1), jnp.concatenate([q1, qa1], axis=1)

    def scores(kj, queries):
        k0 = pl.multiple_of(kj * TK, TK)
        k = jnp.concatenate([k_ref[0, pl.ds(k0, TK), :], kaug_ref[pl.ds(k0, TK), :]], axis=1)
        return _dot_nt(k, q0[queries]), _dot_nt(k, q1[queries])

    def update(state, s, kj, mask):
        vt = vt_ref[kj]
        s0, s1 = s
        if mask is not None:
            s0, s1 = jnp.where(mask, s0, NEG_BIG), jnp.where(mask, s1, NEG_BIG)
        rows = HEAD_DIM + ONES_ROWS
        return _softmax_step(state[0], s0, vt[:rows]), _softmax_step(state[1], s1, vt[rows:])

    init = (_softmax_init(HEAD_DIM), _softmax_init(HEAD_DIM))
    st0, st1 = _sweep_diagonal_last(i, scores, update, init, lambda key, qry: key <= qry)
    o = jnp.concatenate([_softmax_result(st0, HEAD_DIM), _softmax_result(st1, HEAD_DIM)], axis=0).T
    o_ref[0] = (o * _silu(g_ref[0].astype(F32))).astype(BF16)


def _fox_attention(p3, cum_rep):
    b, seq, _ = p3.shape
    gb = GROUP_BLOCKS
    return pl.pallas_call(
        _fox_kernel,
        grid=(b, gb, seq // TQ),
        in_specs=_attn_specs(8 * gb, 9 * gb, 10 * gb, 11 * gb, seq) + [
            pl.BlockSpec((1, seq, LANES), lambda b, u, i: (b, 0, u)),
        ],
        out_specs=pl.BlockSpec((1, TQ, LANES), lambda b, u, i: (b, i, u)),
        out_shape=jax.ShapeDtypeStruct((b, seq, W_GROUP), BF16),
        scratch_shapes=[_vt_scratch(seq, TK, HEAD_DIM), pltpu.VMEM((seq, LANES), BF16)],
        compiler_params=_ATTN_PARAMS,
        name="fox_attention",
    )(p3, p3, p3, p3, cum_rep)


def _out_proj_kernel(ya_ref, yb_ref, yc_ref, w_ref, x_ref, g_ref, o_ref):
    y = (_dot(ya_ref[...], w_ref[0:W_GROUP, :])
         + _dot(yb_ref[...], w_ref[W_GROUP:2 * W_GROUP, :])
         + _dot(yc_ref[...], w_ref[2 * W_GROUP:3 * W_GROUP, :]))
    ms = jnp.mean(y * y, axis=-1, keepdims=True)
    o_ref[...] = x_ref[...] + y * lax.rsqrt(ms + NORM_EPS) * g_ref[...]


def _out_proj(ya, yb, yc, w_out, x2, g_post):
    m = x2.shape[0]
    y_spec = pl.BlockSpec((TM_OUT, W_GROUP), lambda i: (i, 0))
    return pl.pallas_call(
        _out_proj_kernel,
        grid=(m // TM_OUT,),
        in_specs=[
            y_spec, y_spec, y_spec,
            pl.BlockSpec((D_MIX, D_MODEL), lambda i: (0, 0)),
            pl.BlockSpec((TM_OUT, D_MODEL), lambda i: (i, 0)),
            pl.BlockSpec((1, D_MODEL), lambda i: (0, 0)),
        ],
        out_specs=pl.BlockSpec((TM_OUT, D_MODEL), lambda i: (i, 0)),
        out_shape=jax.ShapeDtypeStruct((m, D_MODEL), F32),
        compiler_params=pltpu.CompilerParams(
            dimension_semantics=("arbitrary",), vmem_limit_bytes=VMEM_LIMIT),
        name="out_proj",
    )(ya, yb, yc, w_out, x2, g_post)


def _rope_tables(seq):
    pos = jnp.arange(seq, dtype=F32)
    inv_freq = ROPE_THETA ** (-jnp.arange(0, ROT_DIM, 2, dtype=F32) / ROT_DIM)
    ang = pos[:, None] * inv_freq[None, :]
    cos, sin = jnp.cos(ang), jnp.sin(ang)
    half = ROT_DIM // 2
    ones = jnp.ones((seq, HEAD_DIM - ROT_DIM), F32)
    zeros_rest = jnp.zeros((seq, HEAD_DIM - ROT_DIM), F32)
    zeros_half = jnp.zeros((seq, half), F32)
    rc = jnp.concatenate([cos, cos, ones], axis=-1)
    rs1 = jnp.concatenate([zeros_half, sin, zeros_rest], axis=-1)
    rs2 = jnp.concatenate([-sin, zeros_half, zeros_rest], axis=-1)
    tile = lambda t: jnp.concatenate([t, t], axis=-1)
    return tile(rc), tile(rs1), tile(rs2)


def _q_column_scale():
    scale = LOG2E * HEAD_DIM ** -0.5
    per_group = [1.0] * 12
    per_group[0], per_group[4], per_group[8] = scale, -scale, scale
    return jnp.repeat(jnp.asarray(per_group, F32), W_GROUP)


def _cast_weights_kernel(w_ref, scale_ref, o_ref):
    o_ref[...] = (w_ref[...] * scale_ref[...]).astype(BF16)


def _cast_weights(w_t, row_scale):
    depth = w_t.shape[0]
    w_spec = pl.BlockSpec((None, TN_IN, D_MODEL), lambda l, j: (l, j, 0))
    return pl.pallas_call(
        _cast_weights_kernel,
        grid=(depth, D_MAIN // TN_IN),
        in_specs=[w_spec, pl.BlockSpec((TN_IN, 1), lambda l, j: (j, 0))],
        out_specs=w_spec,
        out_shape=jax.ShapeDtypeStruct((depth, D_MAIN, D_MODEL), BF16),
        compiler_params=pltpu.CompilerParams(
            dimension_semantics=("arbitrary", "arbitrary"), vmem_limit_bytes=VMEM_LIMIT),
        name="cast_weights",
    )(w_t, row_scale)


def _layer(x2, batch, seq, w_all, w_f, f_bias, lam_vec, subln, w_out, g_pre, g_post, layer_idx, tables):
    bias = jnp.pad(f_bias, (0, LANES - N_HEADS_FOX)).reshape(1, LANES)

    p, ff = _in_proj(x2, g_pre.reshape(1, D_MODEL), w_all, layer_idx, w_f, *tables, seq)
    p3 = p.reshape(batch, seq, D_MAIN)
    cum_rep = _cumsum(ff.reshape(batch, seq, LANES), bias)

    lambda_init = 0.8 - 0.6 * math.exp(-0.3 * layer_idx)
    ya = _diff_attention(p3, lam_vec, subln.reshape(1, LANES), lambda_init)
    yb = _sb_attention(p3)
    yc = _fox_attention(p3, cum_rep)

    m = batch * seq
    return _out_proj(ya.reshape(m, W_GROUP), yb.reshape(m, W_GROUP), yc.reshape(m, W_GROUP),
                     w_out, x2, g_post.reshape(1, D_MODEL))


def kernel(x, w_in, forget_bias, diff_lambda, diff_subln, w_out, pre_norm, post_norm):
    batch, seq, d = x.shape
    tables = _rope_tables(seq)
    w_t = jnp.transpose(w_in, (0, 2, 1))
    w_all = _cast_weights(w_t, _q_column_scale().reshape(D_MAIN, 1))
    w_f = jnp.pad(w_t[:, D_MAIN:, :], ((0, 0), (0, LANES - N_HEADS_FOX), (0, 0))).astype(BF16)
    w_out = w_out.astype(BF16)
    x2 = x.reshape(batch * seq, d)
    for l in range(w_in.shape[0]):
        x2 = _layer(x2, batch, seq, w_all, w_f[l], forget_bias[l], diff_lambda[l], diff_subln[l],
                    w_out[l], pre_norm[l], post_norm[l], l, tables)
    return x2.reshape(batch, seq, d)
```
